```python
import math
import jax, jax.numpy as jnp
from jax import lax
import numpy as np

D_MODEL = 4096
BATCH = 32
SEQ = 256
DEPTH = 4
DEC_BATCH = 8
DEC_SEQ = 4096
PAST_LEN = 256

GRID_W = 64
N_MIXERS = 2
N_GDN_LAYERS = (DEPTH + 1) // 2
N_MLA_LAYERS = DEPTH // 2
N_DENSE_LAYERS = (DEPTH + 1) // 2
N_MOE_LAYERS = DEPTH // 2
GDN_HEADS = D_MODEL // 128
GDN_DK = 128
GDN_DV = 128
GDN_KEY_DIM = GDN_HEADS * GDN_DK
GDN_VAL_DIM = GDN_HEADS * GDN_DV
GDN_QKV_DIM = 2 * GDN_KEY_DIM + GDN_VAL_DIM
GDN_IN_DIM = GDN_QKV_DIM + GDN_VAL_DIM + 4 * GDN_HEADS
CONV_K = 5
CHUNK = 64
MLA_HEADS = D_MODEL // 128
Q_LORA = D_MODEL // 4
KV_LORA = 512
NOPE_DIM = 128
ROPE_DIM = 64
V_DIM = 128
MLA_IN_DIM = Q_LORA + KV_LORA + ROPE_DIM
AXIS_ROT = ROPE_DIM // 2
ROPE_THETA = 10000.0
MLA_SCALE = (NOPE_DIM + ROPE_DIM) ** -0.5
Q_BLOCK = 128
D_FF = 5632
N_EXPERTS = 8
TOP_K = 2
D_FF_EXPERT = 1024
EPS = 1e-6

kernel_name = 'hybrid_gdn_mla_diffusion_step'


def rmsnorm(x, g):
    xf = x.astype(jnp.float32)
    y = xf * lax.rsqrt(jnp.mean(xf * xf, axis=-1, keepdims=True) + EPS)
    return (y * g.astype(jnp.float32)).astype(x.dtype)


def l2norm(x):
    return x * lax.rsqrt(jnp.sum(x * x, axis=-1, keepdims=True) + EPS)


def adaln(cvec, w, b):
    mod = (jax.nn.silu(cvec) @ w + b)[:, None, :]
    return jnp.split(mod, 6, axis=-1)


def modulate(h, shift, scale):
    return h * (1.0 + scale) + shift


def centred_conv(x, w):
    return lax.conv_general_dilated(
        x, w[:, None, :].astype(x.dtype), window_strides=(1,),
        padding=[(CONV_K // 2, CONV_K // 2)],
        dimension_numbers=('NWC', 'WIO', 'NWC'), feature_group_count=x.shape[-1])


def delta_chunk_scan(q, k, v, g, beta, s0):
    B, H, L, DK = q.shape
    DV = v.shape[-1]
    n = L // CHUNK

    def blocks(t):
        return t.reshape(B, H, n, CHUNK, *t.shape[3:])

    q, k, v, g, beta = blocks(q), blocks(k), blocks(v), blocks(g), blocks(beta)
    g = jnp.cumsum(g, axis=-1)
    incl = jnp.tril(jnp.ones((CHUNK, CHUNK), dtype=bool))
    strict = jnp.tril(jnp.ones((CHUNK, CHUNK), dtype=bool), -1)
    diff = g[..., :, None] - g[..., None, :]
    decay = jnp.where(incl, jnp.exp(jnp.where(incl, diff, 0.0)), 0.0)
    kb = k * beta[..., None]
    lower = jnp.where(strict, jnp.einsum('bhnid,bhnjd->bhnij', kb, k) * decay, 0.0)
    a_mat = lower + jnp.eye(CHUNK, dtype=lower.dtype)
    rhs = jnp.concatenate([v * beta[..., None], kb * jnp.exp(g)[..., None]], axis=-1)
    sol = lax.linalg.triangular_solve(a_mat, rhs, left_side=True, lower=True, unit_diagonal=True)
    u, w = sol[..., :DV], sol[..., DV:]
    intra = jnp.where(incl, jnp.einsum('bhnid,bhnjd->bhnij', q, k) * decay, 0.0)
    q_dec = q * jnp.exp(g)[..., None]
    k_dec = k * jnp.exp(g[..., -1:] - g)[..., None]
    g_last = jnp.exp(g[..., -1])

    def step(state, xs):
        q_c, k_c, u_c, w_c, a_c, gl = xs
        v_new = u_c - jnp.einsum('bhck,bhkv->bhcv', w_c, state)
        o_c = jnp.einsum('bhck,bhkv->bhcv', q_c, state) + jnp.einsum('bhcs,bhsv->bhcv', a_c, v_new)
        state = state * gl[..., None, None] + jnp.einsum('bhck,bhcv->bhkv', k_c, v_new)
        return state, o_c

    xs = tuple(jnp.moveaxis(t, 2, 0) for t in (q_dec, k_dec, u, w, intra, g_last))
    s_fin, o = lax.scan(step, s0, xs)
    o = jnp.moveaxis(o, 0, 2).reshape(B, H, L, DV)
    return o, s_fin


def bidir_delta(q, k, v, g, beta, s0):
    o_f, s_f = delta_chunk_scan(q, k, v, g[..., 0], beta[..., 0], s0[:, 0])

    def rev(t):
        return jnp.flip(t, axis=2)

    o_b, s_b = delta_chunk_scan(rev(q), rev(k), rev(v), rev(g[..., 1]), rev(beta[..., 1]), s0[:, 1])
    return o_f + rev(o_b), jnp.stack([s_f, s_b], axis=1)


def gdn_mixer(h, s0, w_in, conv_w, a_log, dt_bias, norm_w, w_out):
    B, L, _ = h.shape
    qkv, z, a, b = jnp.split(h @ w_in, [GDN_QKV_DIM, GDN_QKV_DIM + GDN_VAL_DIM,
                                         GDN_QKV_DIM + GDN_VAL_DIM + 2 * GDN_HEADS], axis=-1)
    qkv = jax.nn.silu(centred_conv(qkv, conv_w))
    q, k, v = jnp.split(qkv, [GDN_KEY_DIM, 2 * GDN_KEY_DIM], axis=-1)

    def heads(t, d):
        return t.reshape(B, L, GDN_HEADS, d).transpose(0, 2, 1, 3).astype(jnp.float32)

    q = l2norm(heads(q, GDN_DK)) * (GDN_DK ** -0.5)
    k = l2norm(heads(k, GDN_DK))
    v = heads(v, GDN_DV)
    a = a.reshape(B, L, 2, GDN_HEADS).astype(jnp.float32)
    b = b.reshape(B, L, 2, GDN_HEADS).astype(jnp.float32)
    g = -jnp.exp(a_log.astype(jnp.float32)) * jax.nn.softplus(a + dt_bias.astype(jnp.float32))
    beta = jax.nn.sigmoid(b)
    o, s_fin = bidir_delta(q, k, v, g.transpose(0, 3, 1, 2), beta.transpose(0, 3, 1, 2),
                           s0.astype(jnp.float32))
    o = o.transpose(0, 2, 1, 3)
    zh = z.reshape(B, L, GDN_HEADS, GDN_DV).astype(jnp.float32)
    o = rmsnorm(o, norm_w) * jax.nn.silu(zh)
    out = o.reshape(B, L, GDN_VAL_DIM).astype(h.dtype) @ w_out
    return out, s_fin.astype(h.dtype)


def axial_rope_tables(rows):
    r = jnp.repeat(jnp.arange(rows), GRID_W).astype(jnp.float32)
    col = jnp.tile(jnp.arange(GRID_W), rows).astype(jnp.float32)
    inv = ROPE_THETA ** (-jnp.arange(0, AXIS_ROT, 2, dtype=jnp.float32) / AXIS_ROT)
    ang = jnp.concatenate([r[:, None] * inv, col[:, None] * inv], axis=-1)
    return jnp.cos(ang), jnp.sin(ang)


def apply_axial_rope(x, rope):
    cos, sin = rope
    if x.ndim == 4:
        cos, sin = cos[:, None], sin[:, None]
    half = AXIS_ROT // 2
    xf = x.astype(jnp.float32)
    parts = []
    for ax in range(2):
        seg = xf[..., ax * AXIS_ROT:(ax + 1) * AXIS_ROT]
        x1, x2 = seg[..., :half], seg[..., half:]
        c, s = cos[..., ax * half:(ax + 1) * half], sin[..., ax * half:(ax + 1) * half]
        parts += [x1 * c - x2 * s, x1 * s + x2 * c]
    return jnp.concatenate(parts, axis=-1).astype(x.dtype)


def mla_project(h, w_in, q_norm, w_qb, kv_norm, rope):
    B, L, _ = h.shape
    cq, ckv, kr = jnp.split(h @ w_in, [Q_LORA, Q_LORA + KV_LORA], axis=-1)
    q = (rmsnorm(cq, q_norm) @ w_qb).reshape(B, L, MLA_HEADS, NOPE_DIM + ROPE_DIM)
    qn, qr = q[..., :NOPE_DIM], q[..., NOPE_DIM:]
    ckv = rmsnorm(ckv, kv_norm)
    if rope is not None:
        qr = apply_axial_rope(qr, rope)
        kr = apply_axial_rope(kr, rope)
    return qn, qr, jnp.concatenate([ckv, kr], axis=-1)


def blocked_attention(qn, qr, kn, kr, v):
    B, Lq, H, _ = qn.shape
    nb = Lq // Q_BLOCK

    def split_blocks(t):
        return jnp.moveaxis(t.reshape(B, nb, Q_BLOCK, *t.shape[2:]), 1, 0)

    def one_block(qs):
        qn_b, qr_b = qs
        s = (jnp.einsum('bqhd,bkhd->bhqk', qn_b, kn)
             + jnp.einsum('bqhr,bkr->bhqk', qr_b, kr)).astype(jnp.float32) * MLA_SCALE
        p = jax.nn.softmax(s, axis=-1).astype(v.dtype)
        return jnp.einsum('bhqk,bkhd->bqhd', p, v)

    o = lax.map(one_block, (split_blocks(qn), split_blocks(qr)))
    return jnp.moveaxis(o, 0, 1).reshape(B, Lq, H, v.shape[-1])


def mla_attend(qn, qr, kv, w_kvb, w_out):
    B, Lk, _ = kv.shape
    Lq = qn.shape[1]
    ckv, kr = kv[..., :KV_LORA], kv[..., KV_LORA:]
    kvh = (ckv @ w_kvb).reshape(B, Lk, MLA_HEADS, NOPE_DIM + V_DIM)
    kn, vh = kvh[..., :NOPE_DIM], kvh[..., NOPE_DIM:]
    o = blocked_attention(qn, qr, kn, kr, vh)
    return o.reshape(B, Lq, MLA_HEADS * V_DIM) @ w_out


def swiglu(h, w_gate, w_up, w_down):
    return (jax.nn.silu(h @ w_gate) * (h @ w_up)) @ w_down


def moe(h, w_router, w_gate, w_up, w_down):
    logits = (h @ w_router).astype(jnp.float32)
    top_v, top_i = lax.top_k(logits, TOP_K)
    gates = jax.nn.softmax(top_v, axis=-1)
    combine = jnp.einsum('...ke,...k->...e', jax.nn.one_hot(top_i, N_EXPERTS, dtype=jnp.float32),
                         gates).astype(h.dtype)
    y = jnp.zeros_like(h)
    for e in range(N_EXPERTS):
        y = y + combine[..., e:e + 1] * swiglu(h, w_gate[e], w_up[e], w_down[e])
    return y


def setup_inputs(seed: int = 0) -> dict:
    key = jax.random.key(seed)
    ks = jax.random.split(key, 30)
    f32 = jnp.float32

    def nrm(i, shape, scale):
        return jax.random.normal(ks[i], shape, f32) * scale

    def gain(i, shape):
        return 1.0 + 0.1 * jax.random.normal(ks[i], shape, f32)

    dt = jnp.exp(jax.random.uniform(ks[13], (N_GDN_LAYERS, 2, GDN_HEADS), f32,
                                    math.log(1e-3), math.log(1e-1)))
    return {
        'x_prompt': nrm(0, (BATCH, SEQ, D_MODEL), 1.0),
        'x_sample': nrm(1, (DEC_BATCH, DEC_SEQ, D_MODEL), 1.0),
        'state_gdn': nrm(2, (DEC_BATCH, N_GDN_LAYERS, 2, GDN_HEADS, GDN_DK, GDN_DV), 0.1),
        'cache_mla': nrm(3, (DEC_BATCH, N_MLA_LAYERS, PAST_LEN, KV_LORA + ROPE_DIM), 1.0),
        'c': nrm(4, (DEC_BATCH, D_MODEL), 1.0),
        'c_ctx': nrm(5, (D_MODEL,), 1.0),
        'ada_w': nrm(6, (DEPTH, D_MODEL, 6 * D_MODEL), 0.5 * D_MODEL ** -0.5),
        'ada_b': nrm(7, (DEPTH, 6 * D_MODEL), 0.02),
        'norm1': gain(8, (DEPTH, D_MODEL)),
        'norm2': gain(9, (DEPTH, D_MODEL)),
        'gdn_w_in': nrm(10, (N_GDN_LAYERS, D_MODEL, GDN_IN_DIM), D_MODEL ** -0.5),
        'gdn_conv': nrm(11, (N_GDN_LAYERS, CONV_K, GDN_QKV_DIM), CONV_K ** -0.5),
        'gdn_a_log': jnp.log(jax.random.uniform(ks[12], (N_GDN_LAYERS, 2, GDN_HEADS), f32, 1.0, 16.0)),
        'gdn_dt_bias': dt + jnp.log(-jnp.expm1(-dt)),
        'gdn_norm': gain(14, (N_GDN_LAYERS, GDN_DV)),
        'gdn_w_out': nrm(15, (N_GDN_LAYERS, GDN_VAL_DIM, D_MODEL), GDN_VAL_DIM ** -0.5),
        'mla_w_in': nrm(16, (N_MLA_LAYERS, D_MODEL, MLA_IN_DIM), D_MODEL ** -0.5),
        'mla_q_norm': gain(17, (N_MLA_LAYERS, Q_LORA)),
        'mla_w_qb': nrm(18, (N_MLA_LAYERS, Q_LORA, MLA_HEADS * (NOPE_DIM + ROPE_DIM)), Q_LORA ** -0.5),
        'mla_kv_norm': gain(19, (N_MLA_LAYERS, KV_LORA)),
        'mla_w_kvb': nrm(20, (N_MLA_LAYERS, KV_LORA, MLA_HEADS * (NOPE_DIM + V_DIM)), KV_LORA ** -0.5),
        'mla_w_out': nrm(21, (N_MLA_LAYERS, MLA_HEADS * V_DIM, D_MODEL), (MLA_HEADS * V_DIM) ** -0.5),
        'ffn_w_gate': nrm(22, (N_DENSE_LAYERS, D_MODEL, D_FF), D_MODEL ** -0.5),
        'ffn_w_up': nrm(23, (N_DENSE_LAYERS, D_MODEL, D_FF), D_MODEL ** -0.5),
        'ffn_w_down': nrm(24, (N_DENSE_LAYERS, D_FF, D_MODEL), D_FF ** -0.5),
        'moe_router': nrm(25, (N_MOE_LAYERS, D_MODEL, N_EXPERTS), D_MODEL ** -0.5),
        'moe_w_gate': nrm(26, (N_MOE_LAYERS, N_EXPERTS, D_MODEL, D_FF_EXPERT), D_MODEL ** -0.5),
        'moe_w_up': nrm(27, (N_MOE_LAYERS, N_EXPERTS, D_MODEL, D_FF_EXPERT), D_MODEL ** -0.5),
        'moe_w_down': nrm(28, (N_MOE_LAYERS, N_EXPERTS, D_FF_EXPERT, D_MODEL), D_FF_EXPERT ** -0.5),
        'final_norm': gain(29, (D_MODEL,)),
    }


def reference(x_prompt, x_sample, state_gdn, cache_mla, c, c_ctx, ada_w, ada_b, norm1, norm2,
              gdn_w_in, gdn_conv, gdn_a_log, gdn_dt_bias, gdn_norm, gdn_w_out,
              mla_w_in, mla_q_norm, mla_w_qb, mla_kv_norm, mla_w_kvb, mla_w_out,
              ffn_w_gate, ffn_w_up, ffn_w_down, moe_router, moe_w_gate, moe_w_up, moe_w_down,
              final_norm):
    rows = x_sample.shape[1] // GRID_W
    rope = axial_rope_tables(rows)
    xp, xs = x_prompt, x_sample
    zero_state = jnp.zeros((xp.shape[0], 2, GDN_HEADS, GDN_DK, GDN_DV), xp.dtype)
    new_gdn, new_mla = [], []
    for l in range(DEPTH):
        mp = adaln(c_ctx[None], ada_w[l], ada_b[l])
        ms = adaln(c, ada_w[l], ada_b[l])
        hp = modulate(rmsnorm(xp, norm1[l]), mp[0], mp[1])
        hs = modulate(rmsnorm(xs, norm1[l]), ms[0], ms[1])
        j = l // N_MIXERS
        if l % N_MIXERS == 0:
            gp = (gdn_w_in[j], gdn_conv[j], gdn_a_log[j], gdn_dt_bias[j], gdn_norm[j], gdn_w_out[j])
            op, st = gdn_mixer(hp, zero_state, *gp)
            os_, _ = gdn_mixer(hs, state_gdn[:, j], *gp)
            new_gdn.append(st)
        else:
            mp_args = (mla_w_in[j], mla_q_norm[j], mla_w_qb[j], mla_kv_norm[j])
            qn, qr, kv_ctx = mla_project(hp, *mp_args, None)
            op = mla_attend(qn, qr, kv_ctx, mla_w_kvb[j], mla_w_out[j])
            qn, qr, kv_lat = mla_project(hs, *mp_args, rope)
            kv_all = jnp.concatenate([cache_mla[:, j].astype(kv_lat.dtype), kv_lat], axis=1)
            os_ = mla_attend(qn, qr, kv_all, mla_w_kvb[j], mla_w_out[j])
            new_mla.append(kv_ctx)
        xp = xp + mp[2] * op
        xs = xs + ms[2] * os_
        hp = modulate(rmsnorm(xp, norm2[l]), mp[3], mp[4])
        hs = modulate(rmsnorm(xs, norm2[l]), ms[3], ms[4])
        f = l // 2
        if l % 2 == 0:
            fp = swiglu(hp, ffn_w_gate[f], ffn_w_up[f], ffn_w_down[f])
            fs = swiglu(hs, ffn_w_gate[f], ffn_w_up[f], ffn_w_down[f])
        else:
            fp = moe(hp, moe_router[f], moe_w_gate[f], moe_w_up[f], moe_w_down[f])
            fs = moe(hs, moe_router[f], moe_w_gate[f], moe_w_up[f], moe_w_down[f])
        xp = xp + mp[5] * fp
        xs = xs + ms[5] * fs
    y_prompt = rmsnorm(xp, final_norm)
    y_sample = rmsnorm(xs, final_norm)
    new_state_gdn = jnp.stack(new_gdn, axis=1)
    new_cache_mla = jnp.stack(new_mla, axis=1)
    return (y_prompt, y_sample, new_state_gdn, new_cache_mla)
```

```python
import functools
import math

import jax
import jax.numpy as jnp
from jax import lax
from jax.experimental import pallas as pl
from jax.experimental.pallas import tpu as pltpu

F32 = jnp.float32
BF16 = jnp.bfloat16

HEAD_DIM = 128
ROPE_DIM = 64
AXIS_ROT = ROPE_DIM // 2
ROPE_THETA = 10000.0
GRID_W = 64
CHUNK = 64
TOP_K = 2
EPS = 1e-6
MLA_SCALE = (HEAD_DIM + ROPE_DIM) ** -0.5
LANES = 128
V7X_VMEM_LIMIT = 56 * 1024 * 1024


def _cp(*sem):
    return pltpu.CompilerParams(dimension_semantics=sem, vmem_limit_bytes=V7X_VMEM_LIMIT)


def _tile(dim, pref, align):
    if dim <= pref:
        return dim
    t = (pref // align) * align
    while t >= align:
        if dim % t == 0:
            return t
        t -= align
    return dim


def _dot(a, b):
    return jnp.dot(a, b, preferred_element_type=F32)


def _dot_nt(a, b):
    return lax.dot_general(a, b, (((1,), (1,)), ((), ())), preferred_element_type=F32)


def _dot_tn(a, b):
    return lax.dot_general(a, b, (((0,), (0,)), ((), ())), preferred_element_type=F32)


def _sigmoid(x):
    return 1.0 / (1.0 + jnp.exp(-x))


def _silu(x):
    return x * _sigmoid(x)


def _adaln_kernel(c_ref, w_ref, b_ref, o_ref):
    sc = _silu(c_ref[...]).astype(BF16)
    o_ref[0] = _dot(sc, w_ref[0].astype(BF16)) + b_ref[0]


def adaln_all(cvec, ada_w, ada_b):
    depth, d, n = ada_w.shape
    r = cvec.shape[0]
    tn = _tile(n, 512, LANES)
    return pl.pallas_call(
        _adaln_kernel,
        out_shape=jax.ShapeDtypeStruct((depth, r, n), F32),
        grid=(depth, n // tn),
        in_specs=[
            pl.BlockSpec((r, d), lambda l, j: (0, 0)),
            pl.BlockSpec((1, d, tn), lambda l, j: (l, 0, j)),
            pl.BlockSpec((1, 1, tn), lambda l, j: (l, 0, j)),
        ],
        out_specs=pl.BlockSpec((1, r, tn), lambda l, j: (l, 0, j)),
        compiler_params=_cp("parallel", "parallel"),
        name="adaln",
    )(cvec, ada_w, ada_b.reshape(depth, 1, n))


def _rms(x, g):
    var = jnp.mean(x * x, axis=-1, keepdims=True)
    return x * lax.rsqrt(var + EPS) * g


def _norm_mod_kernel(x_ref, g_ref, m_ref, o_ref, *, shift_row):
    y = _rms(x_ref[0], g_ref[...])
    shift = m_ref[0, shift_row:shift_row + 1, :]
    scale = m_ref[0, shift_row + 1:shift_row + 2, :]
    o_ref[0] = (y * (1.0 + scale) + shift).astype(o_ref.dtype)


def _norm_mod_router_kernel(x_ref, g_ref, m_ref, wr_ref, o_ref, comb_ref, *, shift_row, n_experts):
    y = _rms(x_ref[0], g_ref[...])
    shift = m_ref[0, shift_row:shift_row + 1, :]
    scale = m_ref[0, shift_row + 1:shift_row + 2, :]
    h = (y * (1.0 + scale) + shift).astype(BF16)
    o_ref[0] = h
    logits = _dot(h, wr_ref[...])
    lane = lax.broadcasted_iota(jnp.int32, logits.shape, 1)
    neg = jnp.float32(-jnp.inf)
    lg = jnp.where(lane < n_experts, logits, neg)
    m1 = jnp.max(lg, axis=1, keepdims=True)
    i1 = jnp.min(jnp.where(lg == m1, lane, LANES), axis=1, keepdims=True)
    lg2 = jnp.where(lane == i1, neg, lg)
    m2 = jnp.max(lg2, axis=1, keepdims=True)
    i2 = jnp.min(jnp.where(lg2 == m2, lane, LANES), axis=1, keepdims=True)
    e = jnp.exp(m2 - m1)
    g1 = 1.0 / (1.0 + e)
    g2 = e / (1.0 + e)
    comb = jnp.where(lane == i1, g1, 0.0) + jnp.where(lane == i2, g2, 0.0)
    comb_ref[0] = comb[:, :n_experts]


def norm_mod(x, g, modt, shift_row, router=None):
    nb, s, d = x.shape
    tm = _tile(s, 512, 8)
    x_spec = pl.BlockSpec((1, tm, d), lambda b, i: (b, i, 0))
    g_spec = pl.BlockSpec((1, d), lambda b, i: (0, 0))
    m_spec = pl.BlockSpec((1, 6, d), lambda b, i: (b, 0, 0))
    if router is None:
        return pl.pallas_call(
            functools.partial(_norm_mod_kernel, shift_row=shift_row),
            out_shape=jax.ShapeDtypeStruct((nb, s, d), BF16),
            grid=(nb, s // tm),
            in_specs=[x_spec, g_spec, m_spec],
            out_specs=x_spec,
            compiler_params=_cp("parallel", "parallel"),
            name="norm_mod",
        )(x, g.reshape(1, d), modt)
    n_experts = router.shape[1]
    wr = jnp.zeros((d, LANES), BF16).at[:, :n_experts].set(router.astype(BF16))
    return pl.pallas_call(
        functools.partial(_norm_mod_router_kernel, shift_row=shift_row, n_experts=n_experts),
        out_shape=(jax.ShapeDtypeStruct((nb, s, d), BF16),
                   jax.ShapeDtypeStruct((nb, s, n_experts), F32)),
        grid=(nb, s // tm),
        in_specs=[x_spec, g_spec, m_spec, pl.BlockSpec((d, LANES), lambda b, i: (0, 0))],
        out_specs=(x_spec, pl.BlockSpec((1, tm, n_experts), lambda b, i: (b, i, 0))),
        compiler_params=_cp("parallel", "parallel"),
        name="norm_mod_router",
    )(x, g.reshape(1, d), modt, wr)


def _final_norm_kernel(x_ref, g_ref, o_ref):
    o_ref[0] = _rms(x_ref[0], g_ref[...])


def final_rmsnorm(x, g, nb0, nbn):
    _, s, d = x.shape
    tm = _tile(s, 512, 8)
    return pl.pallas_call(
        _final_norm_kernel,
        out_shape=jax.ShapeDtypeStruct((nbn, s, d), F32),
        grid=(nbn, s // tm),
        in_specs=[pl.BlockSpec((1, tm, d), lambda b, i: (b + nb0, i, 0)),
                  pl.BlockSpec((1, d), lambda b, i: (0, 0))],
        out_specs=pl.BlockSpec((1, tm, d), lambda b, i: (b, i, 0)),
        compiler_params=_cp("parallel", "parallel"),
        name="final_norm",
    )(x, g.reshape(1, d))


def _mm_kernel(x_ref, w_ref, o_ref):
    o_ref[...] = _dot(x_ref[...], w_ref[...]).astype(o_ref.dtype)


def matmul(x, w, out_dtype, tm=1024, tn=1024):
    m, k = x.shape
    n = w.shape[1]
    tm = _tile(m, tm, 8)
    tn = _tile(n, tn, LANES)
    return pl.pallas_call(
        _mm_kernel,
        out_shape=jax.ShapeDtypeStruct((m, n), out_dtype),
        grid=(m // tm, n // tn),
        in_specs=[pl.BlockSpec((tm, k), lambda i, j: (i, 0)),
                  pl.BlockSpec((k, tn), lambda i, j: (0, j))],
        out_specs=pl.BlockSpec((tm, tn), lambda i, j: (i, j)),
        compiler_params=_cp("parallel", "parallel"),
        name="matmul",
    )(x, w)


def _mm_resid_kernel(h_ref, w_ref, x_ref, m_ref, o_ref, *, gate_row):
    gate = m_ref[0, gate_row:gate_row + 1, :]
    o_ref[...] = x_ref[...] + gate * _dot(h_ref[...], w_ref[...])


def matmul_resid(h, w, x, modt, gate_row, s, tm=1024, tn=1024):
    m, k = h.shape
    n = w.shape[1]
    tm = _tile(s, tm, 8)
    tn = _tile(n, tn, LANES)
    per = s // tm
    return pl.pallas_call(
        functools.partial(_mm_resid_kernel, gate_row=gate_row),
        out_shape=jax.ShapeDtypeStruct((m, n), F32),
        grid=(m // tm, n // tn),
        in_specs=[pl.BlockSpec((tm, k), lambda i, j: (i, 0)),
                  pl.BlockSpec((k, tn), lambda i, j: (0, j)),
                  pl.BlockSpec((tm, tn), lambda i, j: (i, j)),
                  pl.BlockSpec((1, 6, tn), lambda i, j: (i // per, 0, j))],
        out_specs=pl.BlockSpec((tm, tn), lambda i, j: (i, j)),
        input_output_aliases={2: 0},
        compiler_params=_cp("parallel", "parallel"),
        name="matmul_resid",
    )(h, w, x, modt)


def _mm_swiglu_kernel(h_ref, wg_ref, wu_ref, o_ref):
    h = h_ref[...]
    a = _dot(h, wg_ref[...])
    b = _dot(h, wu_ref[...])
    o_ref[...] = (_silu(a) * b).astype(o_ref.dtype)


def _mm_swiglu_scaled_kernel(h_ref, wg_ref, wu_ref, c_ref, o_ref, *, cols_per_expert):
    h = h_ref[...]
    a = _dot(h, wg_ref[...])
    b = _dot(h, wu_ref[...])
    tn = o_ref.shape[1]
    e = (pl.program_id(1) * tn) // cols_per_expert
    comb = c_ref[...]
    lane = lax.broadcasted_iota(jnp.int32, comb.shape, 1)
    sel = jnp.sum(jnp.where(lane == e, comb, 0.0), axis=1, keepdims=True)
    o_ref[...] = (_silu(a) * b * sel).astype(o_ref.dtype)


def matmul_swiglu(h, wg, wu, comb=None, cols_per_expert=None, tm=1024, tn=512):
    m, k = h.shape
    n = wg.shape[1]
    tm = _tile(m, tm, 8)
    tn = _tile(n if comb is None else cols_per_expert, tn, LANES)
    in_specs = [pl.BlockSpec((tm, k), lambda i, j: (i, 0)),
                pl.BlockSpec((k, tn), lambda i, j: (0, j)),
                pl.BlockSpec((k, tn), lambda i, j: (0, j))]
    args = [h, wg, wu]
    if comb is None:
        body = _mm_swiglu_kernel
    else:
        body = functools.partial(_mm_swiglu_scaled_kernel, cols_per_expert=cols_per_expert)
        in_specs.append(pl.BlockSpec((tm, comb.shape[1]), lambda i, j: (i, 0)))
        args.append(comb)
    return pl.pallas_call(
        body,
        out_shape=jax.ShapeDtypeStruct((m, n), BF16),
        grid=(m // tm, n // tn),
        in_specs=in_specs,
        out_specs=pl.BlockSpec((tm, tn), lambda i, j: (i, j)),
        compiler_params=_cp("parallel", "parallel"),
        name="matmul_swiglu",
    )(*args)


def _swap16(x):
    lane = lax.broadcasted_iota(jnp.int32, x.shape, 1)
    lo = (lane % 32) < 16
    return jnp.where(lo, pltpu.roll(x, LANES - 16, 1), pltpu.roll(x, 16, 1))


def _mm_rope_kernel(h_ref, w_ref, c_ref, s_ref, o_ref):
    acc = _dot(h_ref[...], w_ref[...])
    cs = c_ref[0]
    sn = s_ref[0]
    for blk in range(o_ref.shape[1] // LANES):
        x = acc[:, blk * LANES:(blk + 1) * LANES]
        if blk % 2 == 1:
            x = x * cs + _swap16(x) * sn
        o_ref[:, blk * LANES:(blk + 1) * LANES] = x.astype(o_ref.dtype)


def matmul_rope(h, w, cos_t, sin_t, s, nbp, tm=1024, tn=1024):
    m, k = h.shape
    n = w.shape[1]
    tm = _tile(s, tm, 8)
    tn = _tile(n, tn, 2 * LANES)
    per = s // tm
    t_spec = pl.BlockSpec((1, tm, LANES), lambda i, j: (jnp.where(i // per >= nbp, 1, 0), i % per, 0))
    return pl.pallas_call(
        _mm_rope_kernel,
        out_shape=jax.ShapeDtypeStruct((m, n), BF16),
        grid=(m // tm, n // tn),
        in_specs=[pl.BlockSpec((tm, k), lambda i, j: (i, 0)),
                  pl.BlockSpec((k, tn), lambda i, j: (0, j)),
                  t_spec, t_spec],
        out_specs=pl.BlockSpec((tm, tn), lambda i, j: (i, j)),
        compiler_params=_cp("parallel", "parallel"),
        name="matmul_rope",
    )(h, w, cos_t, sin_t)


def _gdn_prep_kernel(cur_ref, prev_ref, next_ref, w_ref, o_ref, pad_ref, *, seq_p, seq_s, nbp, d_model):
    b = pl.program_id(0)
    i = pl.program_id(1)
    j = pl.program_id(2)
    tm, tc = cur_ref.shape[1], cur_ref.shape[2]
    k_taps = w_ref.shape[0]
    half = k_taps // 2
    seqlen = jnp.where(b < nbp, seq_p, seq_s)
    row0 = i * tm
    at_start = (row0 % seqlen) == 0
    at_end = ((row0 + tm) % seqlen) == 0
    pad_ref[8:8 + tm, :] = cur_ref[0].astype(F32)
    pad_ref[0:8, :] = jnp.where(at_start, 0.0, prev_ref[0].astype(F32))
    pad_ref[8 + tm:16 + tm, :] = jnp.where(at_end, 0.0, next_ref[0].astype(F32))
    w = w_ref[...]
    acc = jnp.zeros((tm, tc), F32)
    for t in range(k_taps):
        acc = acc + pad_ref[8 - half + t:8 - half + t + tm, :] * w[t:t + 1, :]
    y = _silu(acc)
    col0 = j * tc
    is_qk = col0 < 2 * d_model
    qscale = jnp.where(col0 < d_model, HEAD_DIM ** -0.5, 1.0)
    for hb in range(tc // HEAD_DIM):
        yh = y[:, hb * HEAD_DIM:(hb + 1) * HEAD_DIM]
        ssq = jnp.sum(yh * yh, axis=1, keepdims=True)
        fac = jnp.where(is_qk, lax.rsqrt(ssq + EPS) * qscale, 1.0)
        o_ref[0, :, hb * HEAD_DIM:(hb + 1) * HEAD_DIM] = (yh * fac).astype(o_ref.dtype)


def gdn_prep(qkvz, conv_w, nb, s, seq_p, nbp):
    t, d4 = qkvz.shape
    d = d4 // 4
    x3 = qkvz.reshape(nb, s, d4)
    tm = _tile(min(s, seq_p), 256, 8)
    tc = _tile(d, 512, HEAD_DIM)
    r8 = tm // 8
    last8 = s // 8 - 1
    return pl.pallas_call(
        functools.partial(_gdn_prep_kernel, seq_p=seq_p, seq_s=s, nbp=nbp, d_model=d),
        out_shape=jax.ShapeDtypeStruct((nb, s, 3 * d), BF16),
        grid=(nb, s // tm, 3 * d // tc),
        in_specs=[
            pl.BlockSpec((1, tm, tc), lambda b, i, j: (b, i, j)),
            pl.BlockSpec((1, 8, tc), lambda b, i, j: (b, jnp.maximum(i * r8 - 1, 0), j)),
            pl.BlockSpec((1, 8, tc), lambda b, i, j: (b, jnp.minimum((i + 1) * r8, last8), j)),
            pl.BlockSpec((conv_w.shape[0], tc), lambda b, i, j: (0, j)),
        ],
        out_specs=pl.BlockSpec((1, tm, tc), lambda b, i, j: (b, i, j)),
        scratch_shapes=[pltpu.VMEM((tm + 16, tc), F32)],
        compiler_params=_cp("parallel", "parallel", "parallel"),
        name="gdn_prep",
    )(x3, x3, x3, conv_w)


def _split3(x):
    x1 = x.astype(BF16)
    r = x - x1.astype(F32)
    x2 = r.astype(BF16)
    x3 = (r - x2.astype(F32)).astype(BF16)
    return x1, x2, x3


def _gdn_gates_kernel(ab_ref, alog_ref, dt_ref, o_ref, oc_ref, *, n_heads):
    x = ab_ref[0]
    tm = x.shape[0]
    lane = lax.broadcasted_iota(jnp.int32, x.shape, 1)
    xa = x + dt_ref[...]
    sp = jnp.maximum(xa, 0.0) + jnp.log1p(jnp.exp(-jnp.abs(xa)))
    g = -jnp.exp(alog_ref[...]) * sp
    beta = _sigmoid(x)
    ri = lax.broadcasted_iota(jnp.int32, (tm, tm), 0)
    ci = lax.broadcasted_iota(jnp.int32, (tm, tm), 1)
    same = (ri // CHUNK) == (ci // CHUNK)
    m_pre = jnp.where(same & (ci <= ri), 1.0, 0.0).astype(BF16)
    m_suf = jnp.where(same & (ci >= ri), 1.0, 0.0).astype(BF16)
    pre = jnp.zeros_like(x)
    suf = jnp.zeros_like(x)
    for part in _split3(g):
        pre = pre + _dot(m_pre, part)
        suf = suf + _dot(m_suf, part)
    res = jnp.where(lane < n_heads, pre, jnp.where(lane < 2 * n_heads, suf, beta))
    o_ref[0] = res.T
    oc_ref[0] = res


def gdn_gates(ab, a_log, dt_bias, nb, s):
    t, w = ab.shape
    n_heads = w // 4
    tm = _tile(s, 256, CHUNK)
    pad = jnp.zeros((1, w - 2 * n_heads), F32)
    alog = jnp.concatenate([a_log.reshape(1, 2 * n_heads).astype(F32), pad], axis=1)
    dtb = jnp.concatenate([dt_bias.reshape(1, 2 * n_heads).astype(F32), pad], axis=1)
    return pl.pallas_call(
        functools.partial(_gdn_gates_kernel, n_heads=n_heads),
        out_shape=(jax.ShapeDtypeStruct((nb, w, s), F32),
                   jax.ShapeDtypeStruct((nb, s, w), F32)),
        grid=(nb, s // tm),
        in_specs=[pl.BlockSpec((1, tm, w), lambda b, i: (b, i, 0)),
                  pl.BlockSpec((1, w), lambda b, i: (0, 0)),
                  pl.BlockSpec((1, w), lambda b, i: (0, 0))],
        out_specs=(pl.BlockSpec((1, w, tm), lambda b, i: (b, 0, i)),
                   pl.BlockSpec((1, tm, w), lambda b, i: (b, i, 0))),
        compiler_params=_cp("parallel", "parallel"),
        name="gdn_gates",
    )(ab.reshape(nb, s, w), alog, dtb)


def _gdn_core_kernel(q_ref, k_ref, v_ref, g_ref, gc_ref, s0_ref, o_ref, sfin_ref,
                     state_sc, wq_sc, u_sc, kd_sc, intra_sc, gl_sc, *, hg, nc, n_heads):
    dirn = pl.program_id(0)
    blk = pl.program_id(3)
    nblk = pl.num_programs(3)
    rev = dirn == 1
    c = CHUNK

    @pl.when(blk == 0)
    def _():
        state_sc[...] = s0_ref[0, 0]

    ii = lax.broadcasted_iota(jnp.int32, (c, c), 0)
    jj = lax.broadcasted_iota(jnp.int32, (c, c), 1)
    dmat = jnp.where(rev, jj - ii, ii - jj)
    incl = dmat >= 0
    strict = dmat > 0
    eye = (dmat == 0).astype(F32)

    rows = g_ref[0, 0, 0]
    lane = lax.broadcasted_iota(jnp.int32, (c, 4 * n_heads), 1)
    head0 = dirn * n_heads + pl.program_id(2) * hg

    for cc in range(nc):
        r0 = cc * c
        gtile = gc_ref[0, r0:r0 + c, :]
        for hh in range(hg):
            cs = slice(hh * HEAD_DIM, (hh + 1) * HEAD_DIM)
            q = q_ref[r0:r0 + c, cs]
            k = k_ref[r0:r0 + c, cs]
            v = v_ref[r0:r0 + c, cs].astype(F32)
            kf = k.astype(F32)
            gcol = jnp.sum(jnp.where(lane == head0 + hh, gtile, 0.0), axis=1, keepdims=True)
            bcol = jnp.sum(jnp.where(lane == 2 * n_heads + head0 + hh, gtile, 0.0), axis=1, keepdims=True)
            grow = rows[hh:hh + 1, r0:r0 + c]
            decay = jnp.where(incl, jnp.exp(jnp.where(incl, gcol - grow, 0.0)), 0.0)
            qkk = _dot_nt(jnp.concatenate([q, k], axis=0), k)
            intra = qkk[:c] * decay
            nmat = jnp.where(strict, qkk[c:] * bcol * decay, 0.0)
            r = -nmat
            p = nmat
            for _ in range(int(math.log2(c)) - 1):
                pb = p.astype(BF16)
                p = _dot(pb, pb)
                r = r + p + _dot(r.astype(BF16), p.astype(BF16))
            eg = jnp.exp(gcol)
            rhs = jnp.concatenate([v * bcol, kf * (bcol * eg)], axis=1)
            sol = rhs + _dot(r.astype(BF16), rhs.astype(BF16))
            glast = jnp.where(rev, gcol[0:1], gcol[c - 1:c])
            u_sc[cc, hh] = sol[:, :HEAD_DIM]
            wq_sc[cc, hh, 0:c, :] = sol[:, HEAD_DIM:].astype(BF16)
            wq_sc[cc, hh, c:2 * c, :] = (q.astype(F32) * eg).astype(BF16)
            kd_sc[cc, hh] = (kf * jnp.exp(glast - gcol)).astype(BF16)
            intra_sc[cc, hh] = intra.astype(BF16)
            gl_sc[cc, hh] = jnp.broadcast_to(jnp.exp(glast), (1, HEAD_DIM))

    for step in range(nc):
        cc = jnp.where(rev, nc - 1 - step, step)
        r0 = pl.multiple_of(cc * c, c)
        for hh in range(hg):
            cs = slice(hh * HEAD_DIM, (hh + 1) * HEAD_DIM)
            st = state_sc[hh]
            ws = _dot(wq_sc[cc, hh], st.astype(BF16))
            v_new = u_sc[cc, hh] - ws[:c]
            vb = v_new.astype(BF16)
            o = ws[c:] + _dot(intra_sc[cc, hh], vb)
            state_sc[hh] = st * gl_sc[cc, hh] + _dot_tn(kd_sc[cc, hh], vb)
            o_ref[0, pl.ds(r0, c), cs] = o

    @pl.when(blk == nblk - 1)
    def _():
        sfin_ref[0, 0] = state_sc[...]


def gdn_core(qkv, gates, gates_c, s0, row0, n_seq, seq_len, s, o_prev=None):
    t, d3 = qkv.shape
    d = d3 // 3
    n_heads = d // HEAD_DIM
    hg = 4 if n_heads % 4 == 0 else n_heads
    ngrp = n_heads // hg
    lblk = _tile(seq_len, 256, 2 * CHUNK)
    nc = lblk // CHUNK
    nblk = seq_len // lblk
    wcols = hg * HEAD_DIM
    kofs = d // wcols
    rb0 = row0 // lblk
    per_s = s // lblk
    g5 = gates.reshape(gates.shape[0], 4, ngrp, hg, s)

    def eff(dirn, bk):
        return jnp.where(dirn == 1, nblk - 1 - bk, bk)

    def row_blk(dirn, q, bk):
        return rb0 + q * nblk + eff(dirn, bk)

    def qkv_spec(sec):
        return pl.BlockSpec((lblk, wcols), lambda dr, q, g, bk: (row_blk(dr, q, bk), sec * kofs + g))

    def grow_map(dr, q, g, bk):
        rb = row_blk(dr, q, bk)
        return (rb // per_s, dr, g, 0, rb % per_s)

    def gcol_map(dr, q, g, bk):
        rb = row_blk(dr, q, bk)
        return (rb // per_s, rb % per_s, 0)

    grow_spec = pl.BlockSpec((1, 1, 1, hg, lblk), grow_map)
    gcol_spec = pl.BlockSpec((1, lblk, 4 * n_heads), gcol_map)
    st_spec = pl.BlockSpec((1, 1, hg, HEAD_DIM, HEAD_DIM), lambda dr, q, g, bk: (q, dr, g, 0, 0))
    o_spec = pl.BlockSpec((1, lblk, wcols), lambda dr, q, g, bk: (dr, row_blk(dr, q, bk), g))
    in_specs = [qkv_spec(0), qkv_spec(1), qkv_spec(2), grow_spec, gcol_spec, st_spec]
    args = [qkv, qkv, qkv, g5, gates_c, s0]
    aliases = {}
    body = functools.partial(_gdn_core_kernel, hg=hg, nc=nc, n_heads=n_heads)
    if o_prev is not None:
        in_specs.append(pl.BlockSpec(memory_space=pl.ANY))
        args.append(o_prev)
        aliases = {6: 0}
        inner = body

        def body(q_ref, k_ref, v_ref, g_ref, b_ref, s0_ref, _prev, *rest):
            inner(q_ref, k_ref, v_ref, g_ref, b_ref, s0_ref, *rest)

    return pl.pallas_call(
        body,
        out_shape=(jax.ShapeDtypeStruct((2, t, d), F32),
                   jax.ShapeDtypeStruct((n_seq, 2, n_heads, HEAD_DIM, HEAD_DIM), F32)),
        grid=(2, n_seq, ngrp, nblk),
        in_specs=in_specs,
        out_specs=(o_spec, st_spec),
        scratch_shapes=[
            pltpu.VMEM((hg, HEAD_DIM, HEAD_DIM), F32),
            pltpu.VMEM((nc, hg, 2 * CHUNK, HEAD_DIM), BF16),
            pltpu.VMEM((nc, hg, CHUNK, HEAD_DIM), F32),
            pltpu.VMEM((nc, hg, CHUNK, HEAD_DIM), BF16),
            pltpu.VMEM((nc, hg, CHUNK, CHUNK), BF16),
            pltpu.VMEM((nc, hg, 1, HEAD_DIM), F32),
        ],
        input_output_aliases=aliases,
        compiler_params=_cp("parallel", "parallel", "parallel", "arbitrary"),
        name="gdn_core",
    )(*args)


def _gdn_post_kernel(of_ref, ob_ref, z_ref, nw_ref, o_ref):
    o = of_ref[0] + ob_ref[0]
    z = z_ref[...].astype(F32)
    nw = nw_ref[...]
    for hb in range(o.shape[1] // HEAD_DIM):
        cs = slice(hb * HEAD_DIM, (hb + 1) * HEAD_DIM)
        oh = o[:, cs]
        var = jnp.mean(oh * oh, axis=1, keepdims=True)
        y = oh * lax.rsqrt(var + EPS) * nw
        o_ref[:, cs] = (y * _silu(z[:, cs])).astype(o_ref.dtype)


def gdn_post(o2, qkvz, norm_w):
    _, t, d = o2.shape
    tm = _tile(t, 512, 8)
    tc = _tile(d, 1024, HEAD_DIM)
    zofs = 3 * d // tc
    return pl.pallas_call(
        _gdn_post_kernel,
        out_shape=jax.ShapeDtypeStruct((t, d), BF16),
        grid=(t // tm, d // tc),
        in_specs=[pl.BlockSpec((1, tm, tc), lambda i, j: (0, i, j)),
                  pl.BlockSpec((1, tm, tc), lambda i, j: (1, i, j)),
                  pl.BlockSpec((tm, tc), lambda i, j: (i, zofs + j)),
                  pl.BlockSpec((1, HEAD_DIM), lambda i, j: (0, 0))],
        out_specs=pl.BlockSpec((tm, tc), lambda i, j: (i, j)),
        compiler_params=_cp("parallel", "parallel"),
        name="gdn_post",
    )(o2, o2, qkvz, norm_w.reshape(1, HEAD_DIM).astype(F32))


def _mla_prep_kernel(p_ref, qn_ref, kvn_ref, c_ref, s_ref, cq_ref, kv_ref, *, q_lora, kv_lora):
    p = p_ref[...]
    cq_ref[...] = _rms(p[:, :q_lora], qn_ref[...]).astype(cq_ref.dtype)
    kv_ref[:, :kv_lora] = _rms(p[:, q_lora:q_lora + kv_lora], kvn_ref[...])
    kr = p[:, q_lora + kv_lora:]
    kv_ref[:, kv_lora:] = kr * c_ref[0] + _swap16(kr) * s_ref[0]


def mla_prep(proj, q_norm, kv_norm, cos_t, sin_t, s, nbp, q_lora, kv_lora):
    t, w = proj.shape
    tm = _tile(s, 512, 8)
    per = s // tm
    t_spec = pl.BlockSpec((1, tm, LANES), lambda i: (jnp.where(i // per >= nbp, 1, 0), i % per, 0))
    return pl.pallas_call(
        functools.partial(_mla_prep_kernel, q_lora=q_lora, kv_lora=kv_lora),
        out_shape=(jax.ShapeDtypeStruct((t, q_lora), BF16),
                   jax.ShapeDtypeStruct((t, kv_lora + LANES), F32)),
        grid=(t // tm,),
        in_specs=[pl.BlockSpec((tm, w), lambda i: (i, 0)),
                  pl.BlockSpec((1, q_lora), lambda i: (0, 0)),
                  pl.BlockSpec((1, kv_lora), lambda i: (0, 0)),
                  t_spec, t_spec],
        out_specs=(pl.BlockSpec((tm, q_lora), lambda i: (i, 0)),
                   pl.BlockSpec((tm, kv_lora + LANES), lambda i: (i, 0))),
        compiler_params=_cp("parallel"),
        name="mla_prep",
    )(proj, q_norm.reshape(1, q_lora).astype(F32), kv_norm.reshape(1, kv_lora).astype(F32), cos_t, sin_t)


def _mla_attn_kernel(q_ref, kn_ref, v_ref, kr_ref, o_ref):
    q = q_ref[...]
    s = _dot_nt(q[:, :HEAD_DIM], kn_ref[...]) + _dot_nt(q[:, HEAD_DIM:], kr_ref[...])
    s = s * MLA_SCALE
    m = jnp.max(s, axis=1, keepdims=True)
    p = jnp.exp(s - m)
    l = jnp.sum(p, axis=1, keepdims=True)
    o = _dot(p.astype(BF16), v_ref[...])
    o_ref[...] = (o / l).astype(o_ref.dtype)


def mla_attention(q, kvh, kr, qrow0, n_seq, lq, kvrow0, lk, o_prev=None):
    t = q.shape[0]
    n_heads = q.shape[1] // (2 * HEAD_DIM)
    tq = _tile(lq, 256, 8)
    nq = lq // tq
    qb0 = qrow0 // tq
    kb0 = kvrow0 // lk
    in_specs = [
        pl.BlockSpec((tq, 2 * HEAD_DIM), lambda b, h, i: (qb0 + b * nq + i, h)),
        pl.BlockSpec((lk, HEAD_DIM), lambda b, h, i: (kb0 + b, 2 * h)),
        pl.BlockSpec((lk, HEAD_DIM), lambda b, h, i: (kb0 + b, 2 * h + 1)),
        pl.BlockSpec((lk, LANES), lambda b, h, i: (kb0 + b, 0)),
    ]
    args = [q, kvh, kvh, kr]
    body = _mla_attn_kernel
    aliases = {}
    if o_prev is not None:
        in_specs.append(pl.BlockSpec(memory_space=pl.ANY))
        args.append(o_prev)
        aliases = {4: 0}

        def body(q_ref, kn_ref, v_ref, kr_ref, _prev, o_ref):
            _mla_attn_kernel(q_ref, kn_ref, v_ref, kr_ref, o_ref)

    return pl.pallas_call(
        body,
        out_shape=jax.ShapeDtypeStruct((t, n_heads * HEAD_DIM), BF16),
        grid=(n_seq, n_heads, nq),
        in_specs=in_specs,
        out_specs=pl.BlockSpec((tq, HEAD_DIM), lambda b, h, i: (qb0 + b * nq + i, h)),
        input_output_aliases=aliases,
        compiler_params=_cp("parallel", "parallel", "parallel"),
        name="mla_attention",
    )(*args)


def _rope_tables(s):
    rows = s // GRID_W
    r = jnp.repeat(jnp.arange(rows), GRID_W).astype(F32)
    col = jnp.tile(jnp.arange(GRID_W), rows).astype(F32)
    inv = ROPE_THETA ** (-jnp.arange(0, AXIS_ROT, 2, dtype=F32) / AXIS_ROT)
    ar, ac = r[:, None] * inv, col[:, None] * inv
    zeros = jnp.zeros((s, LANES - ROPE_DIM), F32)
    cos = jnp.concatenate([jnp.cos(ar), jnp.cos(ar), jnp.cos(ac), jnp.cos(ac), zeros], axis=1)
    sin = jnp.concatenate([-jnp.sin(ar), jnp.sin(ar), -jnp.sin(ac), jnp.sin(ac), zeros], axis=1)
    ident_c = jnp.concatenate([jnp.ones((s, ROPE_DIM), F32), zeros], axis=1)
    return jnp.stack([ident_c, cos]), jnp.stack([jnp.zeros((s, LANES), F32), sin])


def kernel(x_prompt, x_sample, state_gdn, cache_mla, c, c_ctx, ada_w, ada_b, norm1, norm2, gdn_w_in, gdn_conv, gdn_a_log, gdn_dt_bias, gdn_norm, gdn_w_out, mla_w_in, mla_q_norm, mla_w_qb, mla_kv_norm, mla_w_kvb, mla_w_out, ffn_w_gate, ffn_w_up, ffn_w_down, moe_router, moe_w_gate, moe_w_up, moe_w_down, final_norm):
    batch, seq, d = x_prompt.shape
    dec_batch, s, _ = x_sample.shape
    depth = ada_w.shape[0]
    assert (batch * seq) % s == 0 and seq % (2 * CHUNK) == 0 and s % GRID_W == 0
    nbp = batch * seq // s
    nb = nbp + dec_batch
    t = nb * s
    tp = nbp * s
    n_heads = d // HEAD_DIM
    past = cache_mla.shape[2]
    q_lora = mla_q_norm.shape[1]
    kv_lora = mla_kv_norm.shape[1]
    lk_s = past + s

    x = jnp.concatenate([x_prompt.reshape(nbp, s, d), x_sample], axis=0).reshape(t, d)

    cvec = jnp.concatenate([jnp.broadcast_to(c_ctx[None], (nbp, d)), c], axis=0)
    rpad = -nb % 8
    cvec = jnp.pad(cvec, ((0, rpad), (0, 0)))
    mod_all = adaln_all(cvec, ada_w, ada_b)[:, :nb].reshape(depth, nb, 6, d)

    cos_t, sin_t = _rope_tables(s)
    zero_state = jnp.zeros((batch, 2, n_heads, HEAD_DIM, HEAD_DIM), F32)
    new_gdn, new_mla = [], []

    for l in range(depth):
        modt = mod_all[l]
        j = l // 2
        h = norm_mod(x.reshape(nb, s, d), norm1[l], modt, 0).reshape(t, d)
        if l % 2 == 0:
            w_in = gdn_w_in[j].astype(BF16)
            qkvz = matmul(h, w_in[:, :4 * d], BF16)
            ab = matmul(h, w_in[:, 4 * d:], F32)
            qkv = gdn_prep(qkvz, gdn_conv[j], nb, s, seq, nbp).reshape(t, 3 * d)
            gates, gates_c = gdn_gates(ab, gdn_a_log[j], gdn_dt_bias[j], nb, s)
            o2, st = gdn_core(qkv, gates, gates_c, zero_state, 0, batch, seq, s)
            o2, _ = gdn_core(qkv, gates, gates_c, state_gdn[:, j], tp, dec_batch, s, s, o_prev=o2)
            new_gdn.append(st)
            mix = gdn_post(o2, qkvz, gdn_norm[j])
            w_out = gdn_w_out[j].astype(BF16)
        else:
            w_in = jnp.pad(mla_w_in[j], ((0, 0), (0, LANES - ROPE_DIM))).astype(BF16)
            proj = matmul(h, w_in, F32, tm=512, tn=w_in.shape[1])
            cq, kv_ent = mla_prep(proj, mla_q_norm[j], mla_kv_norm[j], cos_t, sin_t, s, nbp, q_lora, kv_lora)
            new_mla.append(kv_ent[:tp, :kv_lora + ROPE_DIM].reshape(batch, seq, kv_lora + ROPE_DIM))
            wq = mla_w_qb[j].reshape(q_lora, n_heads, HEAD_DIM + ROPE_DIM)
            wq = jnp.pad(wq, ((0, 0), (0, 0), (0, HEAD_DIM - ROPE_DIM))).reshape(q_lora, n_heads * 2 * HEAD_DIM)
            qf = matmul_rope(cq, wq.astype(BF16), cos_t, sin_t, s, nbp)
            cache = jnp.pad(cache_mla[:, j], ((0, 0), (0, 0), (0, LANES - ROPE_DIM)))
            kv_s = jnp.concatenate([cache, kv_ent[tp:].reshape(dec_batch, s, kv_lora + LANES)], axis=1)
            kv_all = jnp.concatenate([kv_s.reshape(dec_batch * lk_s, kv_lora + LANES), kv_ent[:tp]], axis=0)
            kv_all = kv_all.astype(BF16)
            kvh = matmul(kv_all[:, :kv_lora], mla_w_kvb[j].astype(BF16), BF16)
            kr = kv_all[:, kv_lora:]
            att = mla_attention(qf, kvh, kr, 0, batch, seq, dec_batch * lk_s, seq)
            mix = mla_attention(qf, kvh, kr, tp, dec_batch, s, 0, lk_s, o_prev=att)
            w_out = mla_w_out[j].astype(BF16)
        x = matmul_resid(mix, w_out, x, modt, 2, s)

        if l % 2 == 0:
            h = norm_mod(x.reshape(nb, s, d), norm2[l], modt, 3).reshape(t, d)
            mid = matmul_swiglu(h, ffn_w_gate[j].astype(BF16), ffn_w_up[j].astype(BF16))
            w_down = ffn_w_down[j].astype(BF16)
        else:
            h, comb = norm_mod(x.reshape(nb, s, d), norm2[l], modt, 3, router=moe_router[j])
            h = h.reshape(t, d)
            n_exp, _, fe = moe_w_gate[j].shape
            wg = moe_w_gate[j].transpose(1, 0, 2).reshape(d, n_exp * fe).astype(BF16)
            wu = moe_w_up[j].transpose(1, 0, 2).reshape(d, n_exp * fe).astype(BF16)
            mid = matmul_swiglu(h, wg, wu, comb=comb.reshape(t, n_exp), cols_per_expert=fe)
            w_down = moe_w_down[j].reshape(n_exp * fe, d).astype(BF16)
        x = matmul_resid(mid, w_down, x, modt, 5, s, tm=1024 if mid.shape[1] <= 6144 else 512, tn=512)

    x3 = x.reshape(nb, s, d)
    y_prompt = final_rmsnorm(x3, final_norm, 0, nbp).reshape(batch, seq, d)
    y_sample = final_rmsnorm(x3, final_norm, nbp, dec_batch)
    return (y_prompt, y_sample, jnp.stack(new_gdn, axis=1), jnp.stack(new_mla, axis=1))
```

```python
import functools
import math

import jax
import jax.numpy as jnp
from jax import lax
from jax.experimental import pallas as pl
from jax.experimental.pallas import tpu as pltpu

F32 = jnp.float32
BF16 = jnp.bfloat16

HEAD_DIM = 128
ROPE_DIM = 64
AXIS_ROT = ROPE_DIM // 2
ROPE_THETA = 10000.0
GRID_W = 64
CHUNK = 64
TOP_K = 2
EPS = 1e-6
MLA_SCALE = (HEAD_DIM + ROPE_DIM) ** -0.5
LANES = 128
GDN_HEADS_PER_STEP = 8
GDN_BLOCK_ROWS = 256
MLA_Q_ROWS = 1024
MLA_Q_SUBBLOCKS = 2
LOG2E = 1.4426950408889634
GDN_PROJ_DTYPE = BF16
GDN_QKV_DTYPE = BF16
GDN_O_DTYPE = BF16
SOLVE_PASSES = 3
V7X_VMEM_LIMIT = 56 * 1024 * 1024


def _cp(*sem):
    return pltpu.CompilerParams(dimension_semantics=sem, vmem_limit_bytes=V7X_VMEM_LIMIT)


def _tile(dim, pref, align):
    if dim <= pref:
        return dim
    t = (pref // align) * align
    while t >= align:
        if dim % t == 0:
            return t
        t -= align
    return dim


def _dot(a, b):
    return jnp.dot(a, b, preferred_element_type=F32)


def _dot_nt(a, b):
    return lax.dot_general(a, b, (((1,), (1,)), ((), ())), preferred_element_type=F32)


def _dot_tn(a, b):
    return lax.dot_general(a, b, (((0,), (0,)), ((), ())), preferred_element_type=F32)


def _bdot(a, b):
    return lax.dot_general(a, b, (((2,), (1,)), ((0,), (0,))), preferred_element_type=F32)


def _bdot_nt(a, b):
    return lax.dot_general(a, b, (((2,), (2,)), ((0,), (0,))), preferred_element_type=F32)


def _bdot_tn(a, b):
    return lax.dot_general(a, b, (((1,), (1,)), ((0,), (0,))), preferred_element_type=F32)


def _hi_lo(x):
    hi = x.astype(BF16)
    return hi, (x - hi.astype(F32)).astype(BF16)


def _solve_dot(a, b):
    if SOLVE_PASSES == 1:
        return _bdot(a.astype(BF16), b.astype(BF16))
    a1, a2 = _hi_lo(a)
    b1, b2 = _hi_lo(b)
    return _bdot(a1, b1) + (_bdot(a1, b2) + _bdot(a2, b1))


def _sigmoid(x):
    return 1.0 / (1.0 + jnp.exp(-x))


def _silu(x):
    return x * _sigmoid(x)


def _adaln_kernel(c_ref, w_ref, b_ref, o_ref):
    sc = _silu(c_ref[...]).astype(BF16)
    o_ref[0] = _dot(sc, w_ref[0].astype(BF16)) + b_ref[0]


def adaln_all(cvec, ada_w, ada_b):
    depth, d, n = ada_w.shape
    r = cvec.shape[0]
    tn = _tile(n, 512, LANES)
    return pl.pallas_call(
        _adaln_kernel,
        out_shape=jax.ShapeDtypeStruct((depth, r, n), F32),
        grid=(depth, n // tn),
        in_specs=[
            pl.BlockSpec((r, d), lambda l, j: (0, 0)),
            pl.BlockSpec((1, d, tn), lambda l, j: (l, 0, j)),
            pl.BlockSpec((1, 1, tn), lambda l, j: (l, 0, j)),
        ],
        out_specs=pl.BlockSpec((1, r, tn), lambda l, j: (l, 0, j)),
        compiler_params=_cp("parallel", "parallel"),
        name="adaln",
    )(cvec, ada_w, ada_b.reshape(depth, 1, n))


def _rms(x, g):
    var = jnp.mean(x * x, axis=-1, keepdims=True)
    return x * lax.rsqrt(var + EPS) * g


def _norm_mod_kernel(x_ref, g_ref, m_ref, o_ref, *, shift_row):
    y = _rms(x_ref[0], g_ref[...])
    shift = m_ref[0, shift_row:shift_row + 1, :]
    scale = m_ref[0, shift_row + 1:shift_row + 2, :]
    o_ref[0] = (y * (1.0 + scale) + shift).astype(o_ref.dtype)


def _norm_mod_router_kernel(x_ref, g_ref, m_ref, wr_ref, o_ref, comb_ref, *, shift_row, n_experts):
    y = _rms(x_ref[0], g_ref[...])
    shift = m_ref[0, shift_row:shift_row + 1, :]
    scale = m_ref[0, shift_row + 1:shift_row + 2, :]
    h = (y * (1.0 + scale) + shift).astype(BF16)
    o_ref[0] = h
    logits = _dot(h, wr_ref[...])
    lane = lax.broadcasted_iota(jnp.int32, logits.shape, 1)
    neg = jnp.float32(-jnp.inf)
    lg = jnp.where(lane < n_experts, logits, neg)
    m1 = jnp.max(lg, axis=1, keepdims=True)
    i1 = jnp.min(jnp.where(lg == m1, lane, LANES), axis=1, keepdims=True)
    lg2 = jnp.where(lane == i1, neg, lg)
    m2 = jnp.max(lg2, axis=1, keepdims=True)
    i2 = jnp.min(jnp.where(lg2 == m2, lane, LANES), axis=1, keepdims=True)
    e = jnp.exp(m2 - m1)
    g1 = 1.0 / (1.0 + e)
    g2 = e / (1.0 + e)
    comb = jnp.where(lane == i1, g1, 0.0) + jnp.where(lane == i2, g2, 0.0)
    comb_ref[0] = comb[:, :n_experts]


def norm_mod(x, g, modt, shift_row, router=None):
    nb, s, d = x.shape
    tm = _tile(s, 512, 8)
    x_spec = pl.BlockSpec((1, tm, d), lambda b, i: (b, i, 0))
    g_spec = pl.BlockSpec((1, d), lambda b, i: (0, 0))
    m_spec = pl.BlockSpec((1, 6, d), lambda b, i: (b, 0, 0))
    if router is None:
        return pl.pallas_call(
            functools.partial(_norm_mod_kernel, shift_row=shift_row),
            out_shape=jax.ShapeDtypeStruct((nb, s, d), BF16),
            grid=(nb, s // tm),
            in_specs=[x_spec, g_spec, m_spec],
            out_specs=x_spec,
            compiler_params=_cp("parallel", "parallel"),
            name="norm_mod",
        )(x, g.reshape(1, d), modt)
    n_experts = router.shape[1]
    wr = jnp.zeros((d, LANES), BF16).at[:, :n_experts].set(router.astype(BF16))
    return pl.pallas_call(
        functools.partial(_norm_mod_router_kernel, shift_row=shift_row, n_experts=n_experts),
        out_shape=(jax.ShapeDtypeStruct((nb, s, d), BF16),
                   jax.ShapeDtypeStruct((nb, s, n_experts), F32)),
        grid=(nb, s // tm),
        in_specs=[x_spec, g_spec, m_spec, pl.BlockSpec((d, LANES), lambda b, i: (0, 0))],
        out_specs=(x_spec, pl.BlockSpec((1, tm, n_experts), lambda b, i: (b, i, 0))),
        compiler_params=_cp("parallel", "parallel"),
        name="norm_mod_router",
    )(x, g.reshape(1, d), modt, wr)


def _final_norm_kernel(x_ref, g_ref, o_ref):
    o_ref[0] = _rms(x_ref[0], g_ref[...])


def final_rmsnorm(x, g, nb0, nbn):
    _, s, d = x.shape
    tm = _tile(s, 512, 8)
    return pl.pallas_call(
        _final_norm_kernel,
        out_shape=jax.ShapeDtypeStruct((nbn, s, d), F32),
        grid=(nbn, s // tm),
        in_specs=[pl.BlockSpec((1, tm, d), lambda b, i: (b + nb0, i, 0)),
                  pl.BlockSpec((1, d), lambda b, i: (0, 0))],
        out_specs=pl.BlockSpec((1, tm, d), lambda b, i: (b, i, 0)),
        compiler_params=_cp("parallel", "parallel"),
        name="final_norm",
    )(x, g.reshape(1, d))


def _mm_kernel(x_ref, w_ref, o_ref):
    o_ref[...] = _dot(x_ref[...], w_ref[...]).astype(o_ref.dtype)


def matmul(x, w, out_dtype, tm=1024, tn=1024):
    m, k = x.shape
    n = w.shape[1]
    tm = _tile(m, tm, 8)
    tn = _tile(n, tn, LANES)
    return pl.pallas_call(
        _mm_kernel,
        out_shape=jax.ShapeDtypeStruct((m, n), out_dtype),
        grid=(m // tm, n // tn),
        in_specs=[pl.BlockSpec((tm, k), lambda i, j: (i, 0)),
                  pl.BlockSpec((k, tn), lambda i, j: (0, j))],
        out_specs=pl.BlockSpec((tm, tn), lambda i, j: (i, j)),
        compiler_params=_cp("parallel", "parallel"),
        name="matmul",
    )(x, w)


def _mm_resid_kernel(h_ref, w_ref, x_ref, m_ref, o_ref, *, gate_row):
    gate = m_ref[0, gate_row:gate_row + 1, :]
    o_ref[...] = x_ref[...] + gate * _dot(h_ref[...], w_ref[...])


def matmul_resid(h, w, x, modt, gate_row, s, tm=1024, tn=1024):
    m, k = h.shape
    n = w.shape[1]
    tm = _tile(s, tm, 8)
    tn = _tile(n, tn, LANES)
    per = s // tm
    return pl.pallas_call(
        functools.partial(_mm_resid_kernel, gate_row=gate_row),
        out_shape=jax.ShapeDtypeStruct((m, n), F32),
        grid=(m // tm, n // tn),
        in_specs=[pl.BlockSpec((tm, k), lambda i, j: (i, 0)),
                  pl.BlockSpec((k, tn), lambda i, j: (0, j)),
                  pl.BlockSpec((tm, tn), lambda i, j: (i, j)),
                  pl.BlockSpec((1, 6, tn), lambda i, j: (i // per, 0, j))],
        out_specs=pl.BlockSpec((tm, tn), lambda i, j: (i, j)),
        input_output_aliases={2: 0},
        compiler_params=_cp("parallel", "parallel"),
        name="matmul_resid",
    )(h, w, x, modt)


def _mm_swiglu_kernel(h_ref, wg_ref, wu_ref, o_ref):
    h = h_ref[...]
    a = _dot(h, wg_ref[...])
    b = _dot(h, wu_ref[...])
    o_ref[...] = (_silu(a) * b).astype(o_ref.dtype)


def _mm_swiglu_scaled_kernel(h_ref, wg_ref, wu_ref, c_ref, o_ref, *, cols_per_expert):
    h = h_ref[...]
    a = _dot(h, wg_ref[...])
    b = _dot(h, wu_ref[...])
    tn = o_ref.shape[1]
    e = (pl.program_id(1) * tn) // cols_per_expert
    comb = c_ref[...]
    lane = lax.broadcasted_iota(jnp.int32, comb.shape, 1)
    sel = jnp.sum(jnp.where(lane == e, comb, 0.0), axis=1, keepdims=True)
    o_ref[...] = (_silu(a) * b * sel).astype(o_ref.dtype)


def matmul_swiglu(h, wg, wu, comb=None, cols_per_expert=None, tm=1024, tn=512):
    m, k = h.shape
    n = wg.shape[1]
    tm = _tile(m, tm, 8)
    tn = _tile(n if comb is None else cols_per_expert, tn, LANES)
    in_specs = [pl.BlockSpec((tm, k), lambda i, j: (i, 0)),
                pl.BlockSpec((k, tn), lambda i, j: (0, j)),
                pl.BlockSpec((k, tn), lambda i, j: (0, j))]
    args = [h, wg, wu]
    if comb is None:
        body = _mm_swiglu_kernel
    else:
        body = functools.partial(_mm_swiglu_scaled_kernel, cols_per_expert=cols_per_expert)
        in_specs.append(pl.BlockSpec((tm, comb.shape[1]), lambda i, j: (i, 0)))
        args.append(comb)
    return pl.pallas_call(
        body,
        out_shape=jax.ShapeDtypeStruct((m, n), BF16),
        grid=(m // tm, n // tn),
        in_specs=in_specs,
        out_specs=pl.BlockSpec((tm, tn), lambda i, j: (i, j)),
        compiler_params=_cp("parallel", "parallel"),
        name="matmul_swiglu",
    )(*args)


def _swap16(x):
    lane = lax.broadcasted_iota(jnp.int32, x.shape, 1)
    lo = (lane % 32) < 16
    return jnp.where(lo, pltpu.roll(x, LANES - 16, 1), pltpu.roll(x, 16, 1))


def _mm_rope_kernel(h_ref, w_ref, c_ref, s_ref, o_ref):
    acc = _dot(h_ref[...], w_ref[...]) * (MLA_SCALE * LOG2E)
    cs = c_ref[0]
    sn = s_ref[0]
    for blk in range(o_ref.shape[1] // LANES):
        x = acc[:, blk * LANES:(blk + 1) * LANES]
        if blk % 2 == 1:
            x = x * cs + _swap16(x) * sn
        o_ref[:, blk * LANES:(blk + 1) * LANES] = x.astype(o_ref.dtype)


def matmul_rope(h, w, cos_t, sin_t, s, nbp, tm=1024, tn=1024):
    m, k = h.shape
    n = w.shape[1]
    tm = _tile(s, tm, 8)
    tn = _tile(n, tn, 2 * LANES)
    per = s // tm
    t_spec = pl.BlockSpec((1, tm, LANES), lambda i, j: (jnp.where(i // per >= nbp, 1, 0), i % per, 0))
    return pl.pallas_call(
        _mm_rope_kernel,
        out_shape=jax.ShapeDtypeStruct((m, n), BF16),
        grid=(m // tm, n // tn),
        in_specs=[pl.BlockSpec((tm, k), lambda i, j: (i, 0)),
                  pl.BlockSpec((k, tn), lambda i, j: (0, j)),
                  t_spec, t_spec],
        out_specs=pl.BlockSpec((tm, tn), lambda i, j: (i, j)),
        compiler_params=_cp("parallel", "parallel"),
        name="matmul_rope",
    )(h, w, cos_t, sin_t)


def _gdn_prep_kernel(cur_ref, prev_ref, next_ref, w_ref, o_ref, pad_ref, *, seq_p, seq_s, nbp, d_model):
    b = pl.program_id(0)
    i = pl.program_id(1)
    j = pl.program_id(2)
    tm, tc = cur_ref.shape[1], cur_ref.shape[2]
    k_taps = w_ref.shape[0]
    half = k_taps // 2
    seqlen = jnp.where(b < nbp, seq_p, seq_s)
    row0 = i * tm
    at_start = (row0 % seqlen) == 0
    at_end = ((row0 + tm) % seqlen) == 0
    pad_ref[8:8 + tm, :] = cur_ref[0].astype(F32)
    pad_ref[0:8, :] = jnp.where(at_start, 0.0, prev_ref[0].astype(F32))
    pad_ref[8 + tm:16 + tm, :] = jnp.where(at_end, 0.0, next_ref[0].astype(F32))
    w = w_ref[...]
    acc = jnp.zeros((tm, tc), F32)
    for t in range(k_taps):
        acc = acc + pad_ref[8 - half + t:8 - half + t + tm, :] * w[t:t + 1, :]
    y = _silu(acc)
    col0 = j * tc
    is_qk = col0 < 2 * d_model
    qscale = jnp.where(col0 < d_model, HEAD_DIM ** -0.5, 1.0)
    for hb in range(tc // HEAD_DIM):
        yh = y[:, hb * HEAD_DIM:(hb + 1) * HEAD_DIM]
        ssq = jnp.sum(yh * yh, axis=1, keepdims=True)
        fac = jnp.where(is_qk, lax.rsqrt(ssq + EPS) * qscale, 1.0)
        o_ref[0, :, hb * HEAD_DIM:(hb + 1) * HEAD_DIM] = (yh * fac).astype(o_ref.dtype)


def gdn_prep(qkvz, conv_w, nb, s, seq_p, nbp):
    t, d4 = qkvz.shape
    d = d4 // 4
    x3 = qkvz.reshape(nb, s, d4)
    tm = _tile(min(s, seq_p), 256, 8)
    tc = _tile(d, 512, HEAD_DIM)
    r8 = tm // 8
    last8 = s // 8 - 1
    return pl.pallas_call(
        functools.partial(_gdn_prep_kernel, seq_p=seq_p, seq_s=s, nbp=nbp, d_model=d),
        out_shape=jax.ShapeDtypeStruct((nb, s, 3 * d), GDN_QKV_DTYPE),
        grid=(nb, s // tm, 3 * d // tc),
        in_specs=[
            pl.BlockSpec((1, tm, tc), lambda b, i, j: (b, i, j)),
            pl.BlockSpec((1, 8, tc), lambda b, i, j: (b, jnp.maximum(i * r8 - 1, 0), j)),
            pl.BlockSpec((1, 8, tc), lambda b, i, j: (b, jnp.minimum((i + 1) * r8, last8), j)),
            pl.BlockSpec((conv_w.shape[0], tc), lambda b, i, j: (0, j)),
        ],
        out_specs=pl.BlockSpec((1, tm, tc), lambda b, i, j: (b, i, j)),
        scratch_shapes=[pltpu.VMEM((tm + 16, tc), F32)],
        compiler_params=_cp("parallel", "parallel", "parallel"),
        name="gdn_prep",
    )(x3, x3, x3, conv_w)


def _split3(x):
    x1 = x.astype(BF16)
    r = x - x1.astype(F32)
    x2 = r.astype(BF16)
    x3 = (r - x2.astype(F32)).astype(BF16)
    return x1, x2, x3


def _gdn_gates_kernel(ab_ref, alog_ref, dt_ref, o_ref, oc_ref, *, n_heads):
    x = ab_ref[0]
    tm = x.shape[0]
    lane = lax.broadcasted_iota(jnp.int32, x.shape, 1)
    xa = x + dt_ref[...]
    sp = jnp.maximum(xa, 0.0) + jnp.log1p(jnp.exp(-jnp.abs(xa)))
    g = -jnp.exp(alog_ref[...]) * sp
    beta = _sigmoid(x)
    ri = lax.broadcasted_iota(jnp.int32, (tm, tm), 0)
    ci = lax.broadcasted_iota(jnp.int32, (tm, tm), 1)
    same = (ri // CHUNK) == (ci // CHUNK)
    m_pre = jnp.where(same & (ci <= ri), 1.0, 0.0).astype(BF16)
    m_suf = jnp.where(same & (ci >= ri), 1.0, 0.0).astype(BF16)
    pre = jnp.zeros_like(x)
    suf = jnp.zeros_like(x)
    for part in _split3(g):
        pre = pre + _dot(m_pre, part)
        suf = suf + _dot(m_suf, part)
    res = jnp.where(lane < n_heads, pre, jnp.where(lane < 2 * n_heads, suf, beta))
    o_ref[0] = res.T
    oc_ref[0] = res


def gdn_gates(ab, a_log, dt_bias, nb, s):
    t, w = ab.shape
    n_heads = w // 4
    tm = _tile(s, 256, CHUNK)
    pad = jnp.zeros((1, w - 2 * n_heads), F32)
    alog = jnp.concatenate([a_log.reshape(1, 2 * n_heads).astype(F32), pad], axis=1)
    dtb = jnp.concatenate([dt_bias.reshape(1, 2 * n_heads).astype(F32), pad], axis=1)
    return pl.pallas_call(
        functools.partial(_gdn_gates_kernel, n_heads=n_heads),
        out_shape=(jax.ShapeDtypeStruct((nb, w, s), F32),
                   jax.ShapeDtypeStruct((nb, s, w), F32)),
        grid=(nb, s // tm),
        in_specs=[pl.BlockSpec((1, tm, w), lambda b, i: (b, i, 0)),
                  pl.BlockSpec((1, w), lambda b, i: (0, 0)),
                  pl.BlockSpec((1, w), lambda b, i: (0, 0))],
        out_specs=(pl.BlockSpec((1, w, tm), lambda b, i: (b, 0, i)),
                   pl.BlockSpec((1, tm, w), lambda b, i: (b, i, 0))),
        compiler_params=_cp("parallel", "parallel"),
        name="gdn_gates",
    )(ab.reshape(nb, s, w), alog, dtb)


def _gdn_core_kernel(q_ref, k_ref, v_ref, g_ref, gc_ref, s0_ref, o_ref, sfin_ref,
                     state_sc, wq_sc, u_sc, kd_sc, intra_sc, gl_sc, *, hg, nc, n_heads):
    dirn = pl.program_id(0)
    blk = pl.program_id(3)
    nblk = pl.num_programs(3)
    rev = dirn == 1
    c = CHUNK

    @pl.when(blk == 0)
    def _():
        state_sc[...] = s0_ref[0, 0]

    ii = lax.broadcasted_iota(jnp.int32, (c, c), 0)
    jj = lax.broadcasted_iota(jnp.int32, (c, c), 1)
    dmat = jnp.where(rev, jj - ii, ii - jj)
    incl = dmat >= 0
    strict = dmat > 0

    rows = g_ref[0, 0, 0]
    w4 = 4 * n_heads
    lane0 = dirn * n_heads + pl.program_id(2) * hg

    qs, ks, vs, grows = [], [], [], []
    c_g, c_b, c_eg, c_beg, c_kdm, c_gl = [], [], [], [], [], []
    for cc in range(nc):
        r0 = cc * c
        gt = gc_ref[0, r0:r0 + c, :]
        gg = pltpu.roll(gt, (w4 - lane0) % w4, 1)
        bb = pltpu.roll(gt, 2 * n_heads - lane0, 1)
        eg = jnp.exp(gg)
        beg = bb * eg
        glast = jnp.where(rev, gg[0:1], gg[c - 1:c])
        kdm = jnp.exp(glast - gg)
        glt = jnp.exp(glast)
        for hh in range(hg):
            cs = slice(hh * HEAD_DIM, (hh + 1) * HEAD_DIM)
            sl = slice(hh, hh + 1)
            qs.append(q_ref[r0:r0 + c, cs])
            ks.append(k_ref[r0:r0 + c, cs])
            vs.append(v_ref[r0:r0 + c, cs])
            grows.append(rows[hh:hh + 1, r0:r0 + c])
            c_g.append(gg[:, sl])
            c_b.append(bb[:, sl])
            c_eg.append(eg[:, sl])
            c_beg.append(beg[:, sl])
            c_kdm.append(kdm[:, sl])
            c_gl.append(glt[:, sl])
    q = jnp.stack(qs)
    k = jnp.stack(ks)
    qf = q.astype(F32)
    kf = k.astype(F32)
    vf = jnp.stack(vs).astype(F32)
    gcol = jnp.stack(c_g)
    bcol = jnp.stack(c_b)
    grow = jnp.stack(grows)
    decay = jnp.where(incl, jnp.exp(jnp.where(incl, gcol - grow, 0.0)), 0.0)
    kb = k.astype(BF16)
    qkk = _bdot_nt(jnp.concatenate([q.astype(BF16), kb], axis=1), kb)
    intra = qkk[:, :c] * decay
    nmat = jnp.where(strict, qkk[:, c:] * bcol * decay, 0.0)
    r = -nmat
    p = nmat
    for _ in range(int(math.log2(c)) - 1):
        p = _solve_dot(p, p)
        r = r + p + _solve_dot(r, p)
    rhs =jnp.concatenate([vf * bcol, kf * jnp.stack(c_beg)], axis=2)
    sol = rhs + _solve_dot(r, rhs)
    u_sc[...] = sol[:, :, :HEAD_DIM]
    wq_sc[:, 0:c, :] = sol[:, :, HEAD_DIM:].astype(BF16)
    wq_sc[:, c:2 * c, :] = (qf * jnp.stack(c_eg)).astype(BF16)
    kd_sc[...] = (kf * jnp.stack(c_kdm)).astype(BF16)
    intra_sc[...] = intra.astype(BF16)
    gl_sc[...] = jnp.broadcast_to(jnp.stack(c_gl), gl_sc.shape)

    for step in range(nc):
        cc = jnp.where(rev, nc - 1 - step, step)
        r0 = pl.multiple_of(cc * c, c)
        ps = pl.ds(cc * hg, hg)
        st = state_sc[...]
        ws = _bdot(wq_sc[ps], st.astype(BF16))
        v_new = u_sc[ps] - ws[:, :c]
        vb = v_new.astype(BF16)
        o = ws[:, c:] + _bdot(intra_sc[ps], vb)
        state_sc[...] = st * gl_sc[ps] + _bdot_tn(kd_sc[ps], vb)
        for hh in range(hg):
            o_ref[0, pl.ds(r0, c), hh * HEAD_DIM:(hh + 1) * HEAD_DIM] = o[hh].astype(o_ref.dtype)

    @pl.when(blk == nblk - 1)
    def _():
        sfin_ref[0, 0] = state_sc[...]


def gdn_core(qkv, gates, gates_c, s0, row0, n_seq, seq_len, s, o_prev=None):
    t, d3 = qkv.shape
    d = d3 // 3
    n_heads = d // HEAD_DIM
    hg = GDN_HEADS_PER_STEP if n_heads % GDN_HEADS_PER_STEP == 0 else n_heads
    ngrp = n_heads // hg
    lblk = _tile(seq_len, GDN_BLOCK_ROWS, 2 * CHUNK)
    nc = lblk // CHUNK
    nblk = seq_len // lblk
    wcols = hg * HEAD_DIM
    kofs = d // wcols
    rb0 = row0 // lblk
    per_s = s // lblk
    g5 = gates.reshape(gates.shape[0], 4, ngrp, hg, s)

    def eff(dirn, bk):
        return jnp.where(dirn == 1, nblk - 1 - bk, bk)

    def row_blk(dirn, q, bk):
        return rb0 + q * nblk + eff(dirn, bk)

    def qkv_spec(sec):
        return pl.BlockSpec((lblk, wcols), lambda dr, q, g, bk: (row_blk(dr, q, bk), sec * kofs + g))

    def grow_map(dr, q, g, bk):
        rb = row_blk(dr, q, bk)
        return (rb // per_s, dr, g, 0, rb % per_s)

    def gcol_map(dr, q, g, bk):
        rb = row_blk(dr, q, bk)
        return (rb // per_s, rb % per_s, 0)

    grow_spec = pl.BlockSpec((1, 1, 1, hg, lblk), grow_map)
    gcol_spec = pl.BlockSpec((1, lblk, 4 * n_heads), gcol_map)
    st_spec = pl.BlockSpec((1, 1, hg, HEAD_DIM, HEAD_DIM), lambda dr, q, g, bk: (q, dr, g, 0, 0))
    o_spec = pl.BlockSpec((1, lblk, wcols), lambda dr, q, g, bk: (dr, row_blk(dr, q, bk), g))
    in_specs = [qkv_spec(0), qkv_spec(1), qkv_spec(2), grow_spec, gcol_spec, st_spec]
    args = [qkv, qkv, qkv, g5, gates_c, s0]
    aliases = {}
    body = functools.partial(_gdn_core_kernel, hg=hg, nc=nc, n_heads=n_heads)
    if o_prev is not None:
        in_specs.append(pl.BlockSpec(memory_space=pl.ANY))
        args.append(o_prev)
        aliases = {6: 0}
        inner = body

        def body(q_ref, k_ref, v_ref, g_ref, b_ref, s0_ref, _prev, *rest):
            inner(q_ref, k_ref, v_ref, g_ref, b_ref, s0_ref, *rest)

    return pl.pallas_call(
        body,
        out_shape=(jax.ShapeDtypeStruct((2, t, d), GDN_O_DTYPE),
                   jax.ShapeDtypeStruct((n_seq, 2, n_heads, HEAD_DIM, HEAD_DIM), F32)),
        grid=(2, n_seq, ngrp, nblk),
        in_specs=in_specs,
        out_specs=(o_spec, st_spec),
        scratch_shapes=[
            pltpu.VMEM((hg, HEAD_DIM, HEAD_DIM), F32),
            pltpu.VMEM((nc * hg, 2 * CHUNK, HEAD_DIM), BF16),
            pltpu.VMEM((nc * hg, CHUNK, HEAD_DIM), F32),
            pltpu.VMEM((nc * hg, CHUNK, HEAD_DIM), BF16),
            pltpu.VMEM((nc * hg, CHUNK, CHUNK), BF16),
            pltpu.VMEM((nc * hg, 1, HEAD_DIM), F32),
        ],
        input_output_aliases=aliases,
        compiler_params=_cp("parallel", "parallel", "parallel", "arbitrary"),
        name="gdn_core",
    )(*args)


def _gdn_post_kernel(of_ref, ob_ref, z_ref, nw_ref, o_ref):
    o = of_ref[0].astype(F32) + ob_ref[0].astype(F32)
    z = z_ref[...].astype(F32)
    nw = nw_ref[...]
    for hb in range(o.shape[1] // HEAD_DIM):
        cs = slice(hb * HEAD_DIM, (hb + 1) * HEAD_DIM)
        oh = o[:, cs]
        var = jnp.mean(oh * oh, axis=1, keepdims=True)
        y = oh * lax.rsqrt(var + EPS) * nw
        o_ref[:, cs] = (y * _silu(z[:, cs])).astype(o_ref.dtype)


def gdn_post(o2, qkvz, norm_w):
    _, t, d = o2.shape
    tm = _tile(t, 512, 8)
    tc = _tile(d, 1024, HEAD_DIM)
    zofs = 3 * d // tc
    return pl.pallas_call(
        _gdn_post_kernel,
        out_shape=jax.ShapeDtypeStruct((t, d), BF16),
        grid=(t // tm, d // tc),
        in_specs=[pl.BlockSpec((1, tm, tc), lambda i, j: (0, i, j)),
                  pl.BlockSpec((1, tm, tc), lambda i, j: (1, i, j)),
                  pl.BlockSpec((tm, tc), lambda i, j: (i, zofs + j)),
                  pl.BlockSpec((1, HEAD_DIM), lambda i, j: (0, 0))],
        out_specs=pl.BlockSpec((tm, tc), lambda i, j: (i, j)),
        compiler_params=_cp("parallel", "parallel"),
        name="gdn_post",
    )(o2, o2, qkvz, norm_w.reshape(1, HEAD_DIM).astype(F32))


def _mla_prep_kernel(p_ref, qn_ref, kvn_ref, c_ref, s_ref, cq_ref, kv_ref, *, q_lora, kv_lora):
    p = p_ref[...]
    cq_ref[...] = _rms(p[:, :q_lora], qn_ref[...]).astype(cq_ref.dtype)
    kv_ref[:, :kv_lora] = _rms(p[:, q_lora:q_lora + kv_lora], kvn_ref[...])
    kr = p[:, q_lora + kv_lora:]
    kv_ref[:, kv_lora:] = kr * c_ref[0] + _swap16(kr) * s_ref[0]


def mla_prep(proj, q_norm, kv_norm, cos_t, sin_t, s, nbp, q_lora, kv_lora):
    t, w = proj.shape
    tm = _tile(s, 512, 8)
    per = s // tm
    t_spec = pl.BlockSpec((1, tm, LANES), lambda i: (jnp.where(i // per >= nbp, 1, 0), i % per, 0))
    return pl.pallas_call(
        functools.partial(_mla_prep_kernel, q_lora=q_lora, kv_lora=kv_lora),
        out_shape=(jax.ShapeDtypeStruct((t, q_lora), BF16),
                   jax.ShapeDtypeStruct((t, kv_lora + LANES), F32)),
        grid=(t // tm,),
        in_specs=[pl.BlockSpec((tm, w), lambda i: (i, 0)),
                  pl.BlockSpec((1, q_lora), lambda i: (0, 0)),
                  pl.BlockSpec((1, kv_lora), lambda i: (0, 0)),
                  t_spec, t_spec],
        out_specs=(pl.BlockSpec((tm, q_lora), lambda i: (i, 0)),
                   pl.BlockSpec((tm, kv_lora + LANES), lambda i: (i, 0))),
        compiler_params=_cp("parallel"),
        name="mla_prep",
    )(proj, q_norm.reshape(1, q_lora).astype(F32), kv_norm.reshape(1, kv_lora).astype(F32), cos_t, sin_t)


def _mla_attn_kernel(q_ref, kn_ref, v_ref, kr_ref, o_ref, kf_sc, vf_sc):
    @pl.when(pl.program_id(2) == 0)
    def _():
        kf_sc[:, :HEAD_DIM] = kn_ref[...]
        kf_sc[:, HEAD_DIM:] = kr_ref[...]
        vf_sc[:, :HEAD_DIM] = v_ref[...]
        vf_sc[:, HEAD_DIM:] = jnp.ones((vf_sc.shape[0], HEAD_DIM), BF16)

    tq = q_ref.shape[0]
    sub = tq // MLA_Q_SUBBLOCKS
    ss = [_dot_nt(q_ref[i * sub:(i + 1) * sub, :], kf_sc[...]) for i in range(MLA_Q_SUBBLOCKS)]
    ps = [jnp.exp2(s - jnp.max(s, axis=1, keepdims=True)).astype(BF16) for s in ss]
    for i, p in enumerate(ps):
        ol = _dot(p, vf_sc[...])
        o_ref[i * sub:(i + 1) * sub, :] = (ol[:, :HEAD_DIM] / ol[:, HEAD_DIM:HEAD_DIM + 1]).astype(o_ref.dtype)


def mla_attention(q, kvh, kr, qrow0, n_seq, lq, kvrow0, lk, o_prev=None):
    t = q.shape[0]
    n_heads = q.shape[1] // (2 * HEAD_DIM)
    tq = _tile(lq, MLA_Q_ROWS, 8)
    nq = lq // tq
    qb0 = qrow0 // tq
    kb0 = kvrow0 // lk
    in_specs = [
        pl.BlockSpec((tq, 2 * HEAD_DIM), lambda b, h, i: (qb0 + b * nq + i, h)),
        pl.BlockSpec((lk, HEAD_DIM), lambda b, h, i: (kb0 + b, 2 * h)),
        pl.BlockSpec((lk, HEAD_DIM), lambda b, h, i: (kb0 + b, 2 * h + 1)),
        pl.BlockSpec((lk, LANES), lambda b, h, i: (kb0 + b, 0)),
    ]
    args = [q, kvh, kvh, kr]
    body = _mla_attn_kernel
    aliases = {}
    if o_prev is not None:
        in_specs.append(pl.BlockSpec(memory_space=pl.ANY))
        args.append(o_prev)
        aliases = {4: 0}

        def body(q_ref, kn_ref, v_ref, kr_ref, _prev, o_ref, kf_sc, vf_sc):
            _mla_attn_kernel(q_ref, kn_ref, v_ref, kr_ref, o_ref, kf_sc, vf_sc)

    return pl.pallas_call(
        body,
        out_shape=jax.ShapeDtypeStruct((t, n_heads * HEAD_DIM), BF16),
        grid=(n_seq, n_heads, nq),
        in_specs=in_specs,
        out_specs=pl.BlockSpec((tq, HEAD_DIM), lambda b, h, i: (qb0 + b * nq + i, h)),
        scratch_shapes=[pltpu.VMEM((lk, 2 * HEAD_DIM), BF16), pltpu.VMEM((lk, 2 * HEAD_DIM), BF16)],
        input_output_aliases=aliases,
        compiler_params=_cp("parallel", "parallel", "arbitrary"),
        name="mla_attention",
    )(*args)


def _rope_tables(s):
    rows = s // GRID_W
    r = jnp.repeat(jnp.arange(rows), GRID_W).astype(F32)
    col = jnp.tile(jnp.arange(GRID_W), rows).astype(F32)
    inv = ROPE_THETA ** (-jnp.arange(0, AXIS_ROT, 2, dtype=F32) / AXIS_ROT)
    ar, ac = r[:, None] * inv, col[:, None] * inv
    zeros = jnp.zeros((s, LANES - ROPE_DIM), F32)
    cos = jnp.concatenate([jnp.cos(ar), jnp.cos(ar), jnp.cos(ac), jnp.cos(ac), zeros], axis=1)
    sin = jnp.concatenate([-jnp.sin(ar), jnp.sin(ar), -jnp.sin(ac), jnp.sin(ac), zeros], axis=1)
    ident_c = jnp.concatenate([jnp.ones((s, ROPE_DIM), F32), zeros], axis=1)
    return jnp.stack([ident_c, cos]), jnp.stack([jnp.zeros((s, LANES), F32), sin])


def kernel(x_prompt, x_sample, state_gdn, cache_mla, c, c_ctx, ada_w, ada_b, norm1, norm2, gdn_w_in, gdn_conv, gdn_a_log, gdn_dt_bias, gdn_norm, gdn_w_out, mla_w_in, mla_q_norm, mla_w_qb, mla_kv_norm, mla_w_kvb, mla_w_out, ffn_w_gate, ffn_w_up, ffn_w_down, moe_router, moe_w_gate, moe_w_up, moe_w_down, final_norm):
    batch, seq, d = x_prompt.shape
    dec_batch, s, _ = x_sample.shape
    depth = ada_w.shape[0]
    assert (batch * seq) % s == 0 and seq % (2 * CHUNK) == 0 and s % GRID_W == 0
    nbp = batch * seq // s
    nb = nbp + dec_batch
    t = nb * s
    tp = nbp * s
    n_heads = d // HEAD_DIM
    past = cache_mla.shape[2]
    q_lora = mla_q_norm.shape[1]
    kv_lora = mla_kv_norm.shape[1]
    lk_s = past + s

    x = jnp.concatenate([x_prompt.reshape(nbp, s, d), x_sample], axis=0).reshape(t, d)

    cvec = jnp.concatenate([jnp.broadcast_to(c_ctx[None], (nbp, d)), c], axis=0)
    rpad = -nb % 8
    cvec = jnp.pad(cvec, ((0, rpad), (0, 0)))
    mod_all = adaln_all(cvec, ada_w, ada_b)[:, :nb].reshape(depth, nb, 6, d)

    cos_t, sin_t = _rope_tables(s)
    zero_state = jnp.zeros((batch, 2, n_heads, HEAD_DIM, HEAD_DIM), F32)
    new_gdn, new_mla = [], []

    for l in range(depth):
        modt = mod_all[l]
        j = l // 2
        h = norm_mod(x.reshape(nb, s, d), norm1[l], modt, 0).reshape(t, d)
        if l % 2 == 0:
            w_in = gdn_w_in[j].astype(BF16)
            qkvz = matmul(h, w_in[:, :4 * d], GDN_PROJ_DTYPE)
            ab = matmul(h, w_in[:, 4 * d:], F32)
            qkv = gdn_prep(qkvz, gdn_conv[j], nb, s, seq, nbp).reshape(t, 3 * d)
            gates, gates_c = gdn_gates(ab, gdn_a_log[j], gdn_dt_bias[j], nb, s)
            o2, st = gdn_core(qkv, gates, gates_c, zero_state, 0, batch, seq, s)
            o2, _ = gdn_core(qkv, gates, gates_c, state_gdn[:, j], tp, dec_batch, s, s, o_prev=o2)
            new_gdn.append(st)
            mix = gdn_post(o2, qkvz, gdn_norm[j])
            w_out = gdn_w_out[j].astype(BF16)
        else:
            w_in = jnp.pad(mla_w_in[j], ((0, 0), (0, LANES - ROPE_DIM))).astype(BF16)
            proj = matmul(h, w_in, F32, tm=512, tn=w_in.shape[1])
            cq, kv_ent = mla_prep(proj, mla_q_norm[j], mla_kv_norm[j], cos_t, sin_t, s, nbp, q_lora, kv_lora)
            new_mla.append(kv_ent[:tp, :kv_lora + ROPE_DIM].reshape(batch, seq, kv_lora + ROPE_DIM))
            wq = mla_w_qb[j].reshape(q_lora, n_heads, HEAD_DIM + ROPE_DIM)
            wq = jnp.pad(wq, ((0, 0), (0, 0), (0, HEAD_DIM - ROPE_DIM))).reshape(q_lora, n_heads * 2 * HEAD_DIM)
            qf = matmul_rope(cq, wq.astype(BF16), cos_t, sin_t, s, nbp)
            cache = jnp.pad(cache_mla[:, j], ((0, 0), (0, 0), (0, LANES - ROPE_DIM)))
            kv_s = jnp.concatenate([cache, kv_ent[tp:].reshape(dec_batch, s, kv_lora + LANES)], axis=1)
            kv_all = jnp.concatenate([kv_s.reshape(dec_batch * lk_s, kv_lora + LANES), kv_ent[:tp]], axis=0)
            kv_all = kv_all.astype(BF16)
            kvh = matmul(kv_all[:, :kv_lora], mla_w_kvb[j].astype(BF16), BF16)
            kr = kv_all[:, kv_lora:]
            att = mla_attention(qf, kvh, kr, 0, batch, seq, dec_batch * lk_s, seq)
            mix = mla_attention(qf, kvh, kr, tp, dec_batch, s, 0, lk_s, o_prev=att)
            w_out = mla_w_out[j].astype(BF16)
        x = matmul_resid(mix, w_out, x, modt, 2, s)

        if l % 2 == 0:
            h = norm_mod(x.reshape(nb, s, d), norm2[l], modt, 3).reshape(t, d)
            mid = matmul_swiglu(h, ffn_w_gate[j].astype(BF16), ffn_w_up[j].astype(BF16))
            w_down = ffn_w_down[j].astype(BF16)
        else:
            h, comb = norm_mod(x.reshape(nb, s, d), norm2[l], modt, 3, router=moe_router[j])
            h = h.reshape(t, d)
            n_exp, _, fe = moe_w_gate[j].shape
            wg = moe_w_gate[j].transpose(1, 0, 2).reshape(d, n_exp * fe).astype(BF16)
            wu = moe_w_up[j].transpose(1, 0, 2).reshape(d, n_exp * fe).astype(BF16)
            mid = matmul_swiglu(h, wg, wu, comb=comb.reshape(t, n_exp), cols_per_expert=fe)
            w_down = moe_w_down[j].reshape(n_exp * fe, d).astype(BF16)
        x = matmul_resid(mid, w_down, x, modt, 5, s, tm=1024 if mid.shape[1] <= 6144 else 512, tn=512)

    x3 = x.reshape(nb, s, d)
    y_prompt = final_rmsnorm(x3, final_norm, 0, nbp).reshape(batch, seq, d)
    y_sample = final_rmsnorm(x3, final_norm, nbp, dec_batch)
    return (y_prompt, y_sample, jnp.stack(new_gdn, axis=1), jnp.stack(new_mla, axis=1))
```

```python
import functools
import math

import jax
import jax.numpy as jnp
from jax import lax
from jax.experimental import pallas as pl
from jax.experimental.pallas import tpu as pltpu

F32 = jnp.float32
BF16 = jnp.bfloat16

HEAD_DIM = 128
ROPE_DIM = 64
AXIS_ROT = ROPE_DIM // 2
ROPE_THETA = 10000.0
GRID_W = 64
CHUNK = 64
TOP_K = 2
EPS = 1e-6
MLA_SCALE = (HEAD_DIM + ROPE_DIM) ** -0.5
LANES = 128
GDN_HEADS_PER_STEP = 32
GDN_BLOCK_ROWS = 128
MLA_Q_ROWS = 1024
MLA_Q_SUBBLOCKS = 2
LOG2E = 1.4426950408889634
GDN_PROJ_DTYPE = BF16
GDN_QKV_DTYPE = BF16
GDN_O_DTYPE = BF16
SOLVE_PASSES = 3
V7X_VMEM_LIMIT = 56 * 1024 * 1024


def _cp(*sem):
    return pltpu.CompilerParams(dimension_semantics=sem, vmem_limit_bytes=V7X_VMEM_LIMIT)


def _tile(dim, pref, align):
    if dim <= pref:
        return dim
    t = (pref // align) * align
    while t >= align:
        if dim % t == 0:
            return t
        t -= align
    return dim


def _dot(a, b):
    return jnp.dot(a, b, preferred_element_type=F32)


def _dot_nt(a, b):
    return lax.dot_general(a, b, (((1,), (1,)), ((), ())), preferred_element_type=F32)


def _dot_tn(a, b):
    return lax.dot_general(a, b, (((0,), (0,)), ((), ())), preferred_element_type=F32)


def _bdot(a, b):
    return lax.dot_general(a, b, (((2,), (1,)), ((0,), (0,))), preferred_element_type=F32)


def _bdot_nt(a, b):
    return lax.dot_general(a, b, (((2,), (2,)), ((0,), (0,))), preferred_element_type=F32)


def _bdot_tn(a, b):
    return lax.dot_general(a, b, (((1,), (1,)), ((0,), (0,))), preferred_element_type=F32)


def _hi_lo(x):
    hi = x.astype(BF16)
    return hi, (x - hi.astype(F32)).astype(BF16)


def _solve_dot(a, b):
    if SOLVE_PASSES == 1:
        return _bdot(a.astype(BF16), b.astype(BF16))
    a1, a2 = _hi_lo(a)
    b1, b2 = _hi_lo(b)
    m = a.shape[1]
    t = _bdot(jnp.concatenate([a1, a2], axis=1), b1)
    return t[:, :m] + (_bdot(a1, b2) + t[:, m:])


def _sigmoid(x):
    return 1.0 / (1.0 + jnp.exp(-x))


def _silu(x):
    return x * _sigmoid(x)


def _adaln_kernel(c_ref, w_ref, b_ref, o_ref):
    sc = _silu(c_ref[...]).astype(BF16)
    o_ref[0] = _dot(sc, w_ref[0].astype(BF16)) + b_ref[0]


def adaln_all(cvec, ada_w, ada_b):
    depth, d, n = ada_w.shape
    r = cvec.shape[0]
    tn = _tile(n, 512, LANES)
    return pl.pallas_call(
        _adaln_kernel,
        out_shape=jax.ShapeDtypeStruct((depth, r, n), F32),
        grid=(depth, n // tn),
        in_specs=[
            pl.BlockSpec((r, d), lambda l, j: (0, 0)),
            pl.BlockSpec((1, d, tn), lambda l, j: (l, 0, j)),
            pl.BlockSpec((1, 1, tn), lambda l, j: (l, 0, j)),
        ],
        out_specs=pl.BlockSpec((1, r, tn), lambda l, j: (l, 0, j)),
        compiler_params=_cp("parallel", "parallel"),
        name="adaln",
    )(cvec, ada_w, ada_b.reshape(depth, 1, n))


def _rms(x, g):
    var = jnp.mean(x * x, axis=-1, keepdims=True)
    return x * lax.rsqrt(var + EPS) * g


def _norm_mod_kernel(x_ref, g_ref, m_ref, o_ref, *, shift_row):
    y = _rms(x_ref[0], g_ref[...])
    shift = m_ref[0, shift_row:shift_row + 1, :]
    scale = m_ref[0, shift_row + 1:shift_row + 2, :]
    o_ref[0] = (y * (1.0 + scale) + shift).astype(o_ref.dtype)


def _norm_mod_router_kernel(x_ref, g_ref, m_ref, wr_ref, o_ref, comb_ref, *, shift_row, n_experts):
    y = _rms(x_ref[0], g_ref[...])
    shift = m_ref[0, shift_row:shift_row + 1, :]
    scale = m_ref[0, shift_row + 1:shift_row + 2, :]
    h = (y * (1.0 + scale) + shift).astype(BF16)
    o_ref[0] = h
    logits = _dot(h, wr_ref[...])
    lane = lax.broadcasted_iota(jnp.int32, logits.shape, 1)
    neg = jnp.float32(-jnp.inf)
    lg = jnp.where(lane < n_experts, logits, neg)
    m1 = jnp.max(lg, axis=1, keepdims=True)
    i1 = jnp.min(jnp.where(lg == m1, lane, LANES), axis=1, keepdims=True)
    lg2 = jnp.where(lane == i1, neg, lg)
    m2 = jnp.max(lg2, axis=1, keepdims=True)
    i2 = jnp.min(jnp.where(lg2 == m2, lane, LANES), axis=1, keepdims=True)
    e = jnp.exp(m2 - m1)
    g1 = 1.0 / (1.0 + e)
    g2 = e / (1.0 + e)
    comb = jnp.where(lane == i1, g1, 0.0) + jnp.where(lane == i2, g2, 0.0)
    comb_ref[0] = comb[:, :n_experts]


def norm_mod(x, g, modt, shift_row, router=None):
    nb, s, d = x.shape
    tm = _tile(s, 512, 8)
    x_spec = pl.BlockSpec((1, tm, d), lambda b, i: (b, i, 0))
    g_spec = pl.BlockSpec((1, d), lambda b, i: (0, 0))
    m_spec = pl.BlockSpec((1, 6, d), lambda b, i: (b, 0, 0))
    if router is None:
        return pl.pallas_call(
            functools.partial(_norm_mod_kernel, shift_row=shift_row),
            out_shape=jax.ShapeDtypeStruct((nb, s, d), BF16),
            grid=(nb, s // tm),
            in_specs=[x_spec, g_spec, m_spec],
            out_specs=x_spec,
            compiler_params=_cp("parallel", "parallel"),
            name="norm_mod",
        )(x, g.reshape(1, d), modt)
    n_experts = router.shape[1]
    wr = jnp.zeros((d, LANES), BF16).at[:, :n_experts].set(router.astype(BF16))
    return pl.pallas_call(
        functools.partial(_norm_mod_router_kernel, shift_row=shift_row, n_experts=n_experts),
        out_shape=(jax.ShapeDtypeStruct((nb, s, d), BF16),
                   jax.ShapeDtypeStruct((nb, s, n_experts), F32)),
        grid=(nb, s // tm),
        in_specs=[x_spec, g_spec, m_spec, pl.BlockSpec((d, LANES), lambda b, i: (0, 0))],
        out_specs=(x_spec, pl.BlockSpec((1, tm, n_experts), lambda b, i: (b, i, 0))),
        compiler_params=_cp("parallel", "parallel"),
        name="norm_mod_router",
    )(x, g.reshape(1, d), modt, wr)


def _final_norm_kernel(x_ref, g_ref, o_ref):
    o_ref[0] = _rms(x_ref[0], g_ref[...])


def final_rmsnorm(x, g, nb0, nbn):
    _, s, d = x.shape
    tm = _tile(s, 512, 8)
    return pl.pallas_call(
        _final_norm_kernel,
        out_shape=jax.ShapeDtypeStruct((nbn, s, d), F32),
        grid=(nbn, s // tm),
        in_specs=[pl.BlockSpec((1, tm, d), lambda b, i: (b + nb0, i, 0)),
                  pl.BlockSpec((1, d), lambda b, i: (0, 0))],
        out_specs=pl.BlockSpec((1, tm, d), lambda b, i: (b, i, 0)),
        compiler_params=_cp("parallel", "parallel"),
        name="final_norm",
    )(x, g.reshape(1, d))


def _mm_kernel(x_ref, w_ref, o_ref):
    o_ref[...] = _dot(x_ref[...], w_ref[...]).astype(o_ref.dtype)


def matmul(x, w, out_dtype, tm=1024, tn=1024):
    m, k = x.shape
    n = w.shape[1]
    tm = _tile(m, tm, 8)
    tn = _tile(n, tn, LANES)
    return pl.pallas_call(
        _mm_kernel,
        out_shape=jax.ShapeDtypeStruct((m, n), out_dtype),
        grid=(m // tm, n // tn),
        in_specs=[pl.BlockSpec((tm, k), lambda i, j: (i, 0)),
                  pl.BlockSpec((k, tn), lambda i, j: (0, j))],
        out_specs=pl.BlockSpec((tm, tn), lambda i, j: (i, j)),
        compiler_params=_cp("parallel", "parallel"),
        name="matmul",
    )(x, w)


def _mm_resid_kernel(h_ref, w_ref, x_ref, m_ref, o_ref, *, gate_row):
    gate = m_ref[0, gate_row:gate_row + 1, :]
    o_ref[...] = x_ref[...] + gate * _dot(h_ref[...], w_ref[...])


def matmul_resid(h, w, x, modt, gate_row, s, tm=1024, tn=1024):
    m, k = h.shape
    n = w.shape[1]
    tm = _tile(s, tm, 8)
    tn = _tile(n, tn, LANES)
    per = s // tm
    return pl.pallas_call(
        functools.partial(_mm_resid_kernel, gate_row=gate_row),
        out_shape=jax.ShapeDtypeStruct((m, n), F32),
        grid=(m // tm, n // tn),
        in_specs=[pl.BlockSpec((tm, k), lambda i, j: (i, 0)),
                  pl.BlockSpec((k, tn), lambda i, j: (0, j)),
                  pl.BlockSpec((tm, tn), lambda i, j: (i, j)),
                  pl.BlockSpec((1, 6, tn), lambda i, j: (i // per, 0, j))],
        out_specs=pl.BlockSpec((tm, tn), lambda i, j: (i, j)),
        input_output_aliases={2: 0},
        compiler_params=_cp("parallel", "parallel"),
        name="matmul_resid",
    )(h, w, x, modt)


def _mm_swiglu_kernel(h_ref, wg_ref, wu_ref, o_ref):
    h = h_ref[...]
    a = _dot(h, wg_ref[...])
    b = _dot(h, wu_ref[...])
    o_ref[...] = (_silu(a) * b).astype(o_ref.dtype)


def _mm_swiglu_scaled_kernel(h_ref, wg_ref, wu_ref, c_ref, o_ref, *, cols_per_expert):
    h = h_ref[...]
    a = _dot(h, wg_ref[...])
    b = _dot(h, wu_ref[...])
    tn = o_ref.shape[1]
    e = (pl.program_id(1) * tn) // cols_per_expert
    comb = c_ref[...]
    lane = lax.broadcasted_iota(jnp.int32, comb.shape, 1)
    sel = jnp.sum(jnp.where(lane == e, comb, 0.0), axis=1, keepdims=True)
    o_ref[...] = (_silu(a) * b * sel).astype(o_ref.dtype)


def matmul_swiglu(h, wg, wu, comb=None, cols_per_expert=None, tm=1024, tn=512):
    m, k = h.shape
    n = wg.shape[1]
    tm = _tile(m, tm, 8)
    tn = _tile(n if comb is None else cols_per_expert, tn, LANES)
    in_specs = [pl.BlockSpec((tm, k), lambda i, j: (i, 0)),
                pl.BlockSpec((k, tn), lambda i, j: (0, j)),
                pl.BlockSpec((k, tn), lambda i, j: (0, j))]
    args = [h, wg, wu]
    if comb is None:
        body = _mm_swiglu_kernel
    else:
        body = functools.partial(_mm_swiglu_scaled_kernel, cols_per_expert=cols_per_expert)
        in_specs.append(pl.BlockSpec((tm, comb.shape[1]), lambda i, j: (i, 0)))
        args.append(comb)
    return pl.pallas_call(
        body,
        out_shape=jax.ShapeDtypeStruct((m, n), BF16),
        grid=(m // tm, n // tn),
        in_specs=in_specs,
        out_specs=pl.BlockSpec((tm, tn), lambda i, j: (i, j)),
        compiler_params=_cp("parallel", "parallel"),
        name="matmul_swiglu",
    )(*args)


def _swap16(x):
    lane = lax.broadcasted_iota(jnp.int32, x.shape, 1)
    lo = (lane % 32) < 16
    return jnp.where(lo, pltpu.roll(x, LANES - 16, 1), pltpu.roll(x, 16, 1))


def _mm_rope_kernel(h_ref, w_ref, c_ref, s_ref, o_ref):
    acc = _dot(h_ref[...], w_ref[...]) * (MLA_SCALE * LOG2E)
    cs = c_ref[0]
    sn = s_ref[0]
    for blk in range(o_ref.shape[1] // LANES):
        x = acc[:, blk * LANES:(blk + 1) * LANES]
        if blk % 2 == 1:
            x = x * cs + _swap16(x) * sn
        o_ref[:, blk * LANES:(blk + 1) * LANES] = x.astype(o_ref.dtype)


def matmul_rope(h, w, cos_t, sin_t, s, nbp, tm=1024, tn=1024):
    m, k = h.shape
    n = w.shape[1]
    tm = _tile(s, tm, 8)
    tn = _tile(n, tn, 2 * LANES)
    per = s // tm
    t_spec = pl.BlockSpec((1, tm, LANES), lambda i, j: (jnp.where(i // per >= nbp, 1, 0), i % per, 0))
    return pl.pallas_call(
        _mm_rope_kernel,
        out_shape=jax.ShapeDtypeStruct((m, n), BF16),
        grid=(m // tm, n // tn),
        in_specs=[pl.BlockSpec((tm, k), lambda i, j: (i, 0)),
                  pl.BlockSpec((k, tn), lambda i, j: (0, j)),
                  t_spec, t_spec],
        out_specs=pl.BlockSpec((tm, tn), lambda i, j: (i, j)),
        compiler_params=_cp("parallel", "parallel"),
        name="matmul_rope",
    )(h, w, cos_t, sin_t)


def _gdn_prep_kernel(cur_ref, prev_ref, next_ref, w_ref, o_ref, pad_ref, *, seq_p, seq_s, nbp, d_model):
    b = pl.program_id(0)
    i = pl.program_id(1)
    j = pl.program_id(2)
    tm, tc = cur_ref.shape[1], cur_ref.shape[2]
    k_taps = w_ref.shape[0]
    half = k_taps // 2
    seqlen = jnp.where(b < nbp, seq_p, seq_s)
    row0 = i * tm
    at_start = (row0 % seqlen) == 0
    at_end = ((row0 + tm) % seqlen) == 0
    pad_ref[8:8 + tm, :] = cur_ref[0].astype(F32)
    pad_ref[0:8, :] = jnp.where(at_start, 0.0, prev_ref[0].astype(F32))
    pad_ref[8 + tm:16 + tm, :] = jnp.where(at_end, 0.0, next_ref[0].astype(F32))
    w = w_ref[...]
    acc = jnp.zeros((tm, tc), F32)
    for t in range(k_taps):
        acc = acc + pad_ref[8 - half + t:8 - half + t + tm, :] * w[t:t + 1, :]
    y = _silu(acc)
    col0 = j * tc
    is_qk = col0 < 2 * d_model
    qscale = jnp.where(col0 < d_model, HEAD_DIM ** -0.5, 1.0)
    for hb in range(tc // HEAD_DIM):
        yh = y[:, hb * HEAD_DIM:(hb + 1) * HEAD_DIM]
        ssq = jnp.sum(yh * yh, axis=1, keepdims=True)
        fac = jnp.where(is_qk, lax.rsqrt(ssq + EPS) * qscale, 1.0)
        o_ref[0, :, hb * HEAD_DIM:(hb + 1) * HEAD_DIM] = (yh * fac).astype(o_ref.dtype)


def gdn_prep(qkvz, conv_w, nb, s, seq_p, nbp):
    t, d4 = qkvz.shape
    d = d4 // 4
    x3 = qkvz.reshape(nb, s, d4)
    tm = _tile(min(s, seq_p), 256, 8)
    tc = _tile(d, 512, HEAD_DIM)
    r8 = tm // 8
    last8 = s // 8 - 1
    return pl.pallas_call(
        functools.partial(_gdn_prep_kernel, seq_p=seq_p, seq_s=s, nbp=nbp, d_model=d),
        out_shape=jax.ShapeDtypeStruct((nb, s, 3 * d), GDN_QKV_DTYPE),
        grid=(nb, s // tm, 3 * d // tc),
        in_specs=[
            pl.BlockSpec((1, tm, tc), lambda b, i, j: (b, i, j)),
            pl.BlockSpec((1, 8, tc), lambda b, i, j: (b, jnp.maximum(i * r8 - 1, 0), j)),
            pl.BlockSpec((1, 8, tc), lambda b, i, j: (b, jnp.minimum((i + 1) * r8, last8), j)),
            pl.BlockSpec((conv_w.shape[0], tc), lambda b, i, j: (0, j)),
        ],
        out_specs=pl.BlockSpec((1, tm, tc), lambda b, i, j: (b, i, j)),
        scratch_shapes=[pltpu.VMEM((tm + 16, tc), F32)],
        compiler_params=_cp("parallel", "parallel", "parallel"),
        name="gdn_prep",
    )(x3, x3, x3, conv_w)


def _split3(x):
    x1 = x.astype(BF16)
    r = x - x1.astype(F32)
    x2 = r.astype(BF16)
    x3 = (r - x2.astype(F32)).astype(BF16)
    return x1, x2, x3


def _gdn_gates_kernel(ab_ref, alog_ref, dt_ref, o_ref, oc_ref, *, n_heads):
    x = ab_ref[0]
    tm = x.shape[0]
    lane = lax.broadcasted_iota(jnp.int32, x.shape, 1)
    xa = x + dt_ref[...]
    sp = jnp.maximum(xa, 0.0) + jnp.log1p(jnp.exp(-jnp.abs(xa)))
    g = -jnp.exp(alog_ref[...]) * sp
    beta = _sigmoid(x)
    ri = lax.broadcasted_iota(jnp.int32, (tm, tm), 0)
    ci = lax.broadcasted_iota(jnp.int32, (tm, tm), 1)
    same = (ri // CHUNK) == (ci // CHUNK)
    m_pre = jnp.where(same & (ci <= ri), 1.0, 0.0).astype(BF16)
    m_suf = jnp.where(same & (ci >= ri), 1.0, 0.0).astype(BF16)
    pre = jnp.zeros_like(x)
    suf = jnp.zeros_like(x)
    for part in _split3(g):
        pre = pre + _dot(m_pre, part)
        suf = suf + _dot(m_suf, part)
    res = jnp.where(lane < n_heads, pre, jnp.where(lane < 2 * n_heads, suf, beta))
    o_ref[0] = res.T
    oc_ref[0] = res


def gdn_gates(ab, a_log, dt_bias, nb, s):
    t, w = ab.shape
    n_heads = w // 4
    tm = _tile(s, 256, CHUNK)
    pad = jnp.zeros((1, w - 2 * n_heads), F32)
    alog = jnp.concatenate([a_log.reshape(1, 2 * n_heads).astype(F32), pad], axis=1)
    dtb = jnp.concatenate([dt_bias.reshape(1, 2 * n_heads).astype(F32), pad], axis=1)
    return pl.pallas_call(
        functools.partial(_gdn_gates_kernel, n_heads=n_heads),
        out_shape=(jax.ShapeDtypeStruct((nb, w, s), F32),
                   jax.ShapeDtypeStruct((nb, s, w), F32)),
        grid=(nb, s // tm),
        in_specs=[pl.BlockSpec((1, tm, w), lambda b, i: (b, i, 0)),
                  pl.BlockSpec((1, w), lambda b, i: (0, 0)),
                  pl.BlockSpec((1, w), lambda b, i: (0, 0))],
        out_specs=(pl.BlockSpec((1, w, tm), lambda b, i: (b, 0, i)),
                   pl.BlockSpec((1, tm, w), lambda b, i: (b, i, 0))),
        compiler_params=_cp("parallel", "parallel"),
        name="gdn_gates",
    )(ab.reshape(nb, s, w), alog, dtb)


def _gdn_core_kernel(q_ref, k_ref, v_ref, g_ref, gc_ref, s0_ref, o_ref, sfin_ref,
                     state_sc, wq_sc, u_sc, kd_sc, intra_sc, gl_sc, *, hg, nc, n_heads):
    dirn = pl.program_id(0)
    blk = pl.program_id(3)
    nblk = pl.num_programs(3)
    rev = dirn == 1
    c = CHUNK

    @pl.when(blk == 0)
    def _():
        state_sc[...] = s0_ref[0, 0]

    ii = lax.broadcasted_iota(jnp.int32, (c, c), 0)
    jj = lax.broadcasted_iota(jnp.int32, (c, c), 1)
    dmat = jnp.where(rev, jj - ii, ii - jj)
    incl = dmat >= 0
    strict = dmat > 0

    rows = g_ref[0, 0, 0]
    w4 = 4 * n_heads
    lane0 = dirn * n_heads + pl.program_id(2) * hg

    qs, ks, vs, grows = [], [], [], []
    c_g, c_b, c_eg, c_beg, c_kdm, c_gl = [], [], [], [], [], []
    for cc in range(nc):
        r0 = cc * c
        gt = gc_ref[0, r0:r0 + c, :]
        gg = pltpu.roll(gt, (w4 - lane0) % w4, 1)
        bb = pltpu.roll(gt, 2 * n_heads - lane0, 1)
        eg = jnp.exp(gg)
        beg = bb * eg
        glast = jnp.where(rev, gg[0:1], gg[c - 1:c])
        kdm = jnp.exp(glast - gg)
        glt = jnp.exp(glast)
        for hh in range(hg):
            cs = slice(hh * HEAD_DIM, (hh + 1) * HEAD_DIM)
            sl = slice(hh, hh + 1)
            qs.append(q_ref[r0:r0 + c, cs])
            ks.append(k_ref[r0:r0 + c, cs])
            vs.append(v_ref[r0:r0 + c, cs])
            grows.append(rows[hh:hh + 1, r0:r0 + c])
            c_g.append(gg[:, sl])
            c_b.append(bb[:, sl])
            c_eg.append(eg[:, sl])
            c_beg.append(beg[:, sl])
            c_kdm.append(kdm[:, sl])
            c_gl.append(glt[:, sl])
    q = jnp.stack(qs)
    k = jnp.stack(ks)
    qf = q.astype(F32)
    kf = k.astype(F32)
    vf = jnp.stack(vs).astype(F32)
    gcol = jnp.stack(c_g)
    bcol = jnp.stack(c_b)
    grow = jnp.stack(grows)
    decay = jnp.where(incl, jnp.exp(jnp.where(incl, gcol - grow, 0.0)), 0.0)
    kb = k.astype(BF16)
    qkk = _bdot_nt(jnp.concatenate([q.astype(BF16), kb], axis=1), kb)
    intra = qkk[:, :c] * decay
    nmat = jnp.where(strict, qkk[:, c:] * bcol * decay, 0.0)
    mneg = -nmat
    w = jnp.concatenate([_solve_dot(mneg, mneg), mneg], axis=2)
    upper = lax.broadcasted_iota(jnp.int32, (c, 2 * c), 1) >= c
    for _ in range(int(math.log2(c)) - 2):
        w = _solve_dot(w[:, :, :c], w) + jnp.where(upper, w + pltpu.roll(w, c, 2), 0.0)
    qk = w[:, :, :c]
    rk = w[:, :, c:]
    r = rk + qk + _solve_dot(qk, rk)
    rhs = jnp.concatenate([vf * bcol, kf * jnp.stack(c_beg)], axis=2)
    sol = rhs + _bdot(r.astype(BF16), rhs.astype(BF16))
    u_sc[...] = sol[:, :, :HEAD_DIM]
    wq_sc[:, 0:c, :] = sol[:, :, HEAD_DIM:].astype(BF16)
    wq_sc[:, c:2 * c, :] = (qf * jnp.stack(c_eg)).astype(BF16)
    kd_sc[...] = (kf * jnp.stack(c_kdm)).astype(BF16)
    intra_sc[...] = intra.astype(BF16)
    gl_sc[...] = jnp.broadcast_to(jnp.stack(c_gl), gl_sc.shape)

    for step in range(nc):
        cc = jnp.where(rev, nc - 1 - step, step)
        r0 = pl.multiple_of(cc * c, c)
        ps = pl.ds(cc * hg, hg)
        st = state_sc[...]
        ws = _bdot(wq_sc[ps], st.astype(BF16))
        v_new = u_sc[ps] - ws[:, :c]
        vb = v_new.astype(BF16)
        o = ws[:, c:] + _bdot(intra_sc[ps], vb)
        state_sc[...] = st * gl_sc[ps] + _bdot_tn(kd_sc[ps], vb)
        for hh in range(hg):
            o_ref[0, pl.ds(r0, c), hh * HEAD_DIM:(hh + 1) * HEAD_DIM] = o[hh].astype(o_ref.dtype)

    @pl.when(blk == nblk - 1)
    def _():
        sfin_ref[0, 0] = state_sc[...]


def gdn_core(qkv, gates, gates_c, s0, row0, n_seq, seq_len, s, o_prev=None):
    t, d3 = qkv.shape
    d = d3 // 3
    n_heads = d // HEAD_DIM
    hg = GDN_HEADS_PER_STEP if n_heads % GDN_HEADS_PER_STEP == 0 else n_heads
    ngrp = n_heads // hg
    lblk = _tile(seq_len, GDN_BLOCK_ROWS, 2 * CHUNK)
    nc = lblk // CHUNK
    nblk = seq_len // lblk
    wcols = hg * HEAD_DIM
    kofs = d // wcols
    rb0 = row0 // lblk
    per_s = s // lblk
    g5 = gates.reshape(gates.shape[0], 4, ngrp, hg, s)

    def eff(dirn, bk):
        return jnp.where(dirn == 1, nblk - 1 - bk, bk)

    def row_blk(dirn, q, bk):
        return rb0 + q * nblk + eff(dirn, bk)

    def qkv_spec(sec):
        return pl.BlockSpec((lblk, wcols), lambda dr, q, g, bk: (row_blk(dr, q, bk), sec * kofs + g))

    def grow_map(dr, q, g, bk):
        rb = row_blk(dr, q, bk)
        return (rb // per_s, dr, g, 0, rb % per_s)

    def gcol_map(dr, q, g, bk):
        rb = row_blk(dr, q, bk)
        return (rb // per_s, rb % per_s, 0)

    grow_spec = pl.BlockSpec((1, 1, 1, hg, lblk), grow_map)
    gcol_spec = pl.BlockSpec((1, lblk, 4 * n_heads), gcol_map)
    st_spec = pl.BlockSpec((1, 1, hg, HEAD_DIM, HEAD_DIM), lambda dr, q, g, bk: (q, dr, g, 0, 0))
    o_spec = pl.BlockSpec((1, lblk, wcols), lambda dr, q, g, bk: (dr, row_blk(dr, q, bk), g))
    in_specs = [qkv_spec(0), qkv_spec(1), qkv_spec(2), grow_spec, gcol_spec, st_spec]
    args = [qkv, qkv, qkv, g5, gates_c, s0]
    aliases = {}
    body = functools.partial(_gdn_core_kernel, hg=hg, nc=nc, n_heads=n_heads)
    if o_prev is not None:
        in_specs.append(pl.BlockSpec(memory_space=pl.ANY))
        args.append(o_prev)
        aliases = {6: 0}
        inner = body

        def body(q_ref, k_ref, v_ref, g_ref, b_ref, s0_ref, _prev, *rest):
            inner(q_ref, k_ref, v_ref, g_ref, b_ref, s0_ref, *rest)

    return pl.pallas_call(
        body,
        out_shape=(jax.ShapeDtypeStruct((2, t, d), GDN_O_DTYPE),
                   jax.ShapeDtypeStruct((n_seq, 2, n_heads, HEAD_DIM, HEAD_DIM), F32)),
        grid=(2, n_seq, ngrp, nblk),
        in_specs=in_specs,
        out_specs=(o_spec, st_spec),
        scratch_shapes=[
            pltpu.VMEM((hg, HEAD_DIM, HEAD_DIM), F32),
            pltpu.VMEM((nc * hg, 2 * CHUNK, HEAD_DIM), BF16),
            pltpu.VMEM((nc * hg, CHUNK, HEAD_DIM), F32),
            pltpu.VMEM((nc * hg, CHUNK, HEAD_DIM), BF16),
            pltpu.VMEM((nc * hg, CHUNK, CHUNK), BF16),
            pltpu.VMEM((nc * hg, 1, HEAD_DIM), F32),
        ],
        input_output_aliases=aliases,
        compiler_params=_cp("parallel", "parallel", "parallel", "arbitrary"),
        name="gdn_core",
    )(*args)


def _gdn_post_kernel(of_ref, ob_ref, z_ref, nw_ref, o_ref):
    o = of_ref[0].astype(F32) + ob_ref[0].astype(F32)
    z = z_ref[...].astype(F32)
    nw = nw_ref[...]
    for hb in range(o.shape[1] // HEAD_DIM):
        cs = slice(hb * HEAD_DIM, (hb + 1) * HEAD_DIM)
        oh = o[:, cs]
        var = jnp.mean(oh * oh, axis=1, keepdims=True)
        y = oh * lax.rsqrt(var + EPS) * nw
        o_ref[:, cs] = (y * _silu(z[:, cs])).astype(o_ref.dtype)


def gdn_post(o2, qkvz, norm_w):
    _, t, d = o2.shape
    tm = _tile(t, 512, 8)
    tc = _tile(d, 1024, HEAD_DIM)
    zofs = 3 * d // tc
    return pl.pallas_call(
        _gdn_post_kernel,
        out_shape=jax.ShapeDtypeStruct((t, d), BF16),
        grid=(t // tm, d // tc),
        in_specs=[pl.BlockSpec((1, tm, tc), lambda i, j: (0, i, j)),
                  pl.BlockSpec((1, tm, tc), lambda i, j: (1, i, j)),
                  pl.BlockSpec((tm, tc), lambda i, j: (i, zofs + j)),
                  pl.BlockSpec((1, HEAD_DIM), lambda i, j: (0, 0))],
        out_specs=pl.BlockSpec((tm, tc), lambda i, j: (i, j)),
        compiler_params=_cp("parallel", "parallel"),
        name="gdn_post",
    )(o2, o2, qkvz, norm_w.reshape(1, HEAD_DIM).astype(F32))


def _mla_prep_kernel(p_ref, qn_ref, kvn_ref, c_ref, s_ref, cq_ref, kv_ref, *, q_lora, kv_lora):
    p = p_ref[...]
    cq_ref[...] = _rms(p[:, :q_lora], qn_ref[...]).astype(cq_ref.dtype)
    kv_ref[:, :kv_lora] = _rms(p[:, q_lora:q_lora + kv_lora], kvn_ref[...])
    kr = p[:, q_lora + kv_lora:]
    kv_ref[:, kv_lora:] = kr * c_ref[0] + _swap16(kr) * s_ref[0]


def mla_prep(proj, q_norm, kv_norm, cos_t, sin_t, s, nbp, q_lora, kv_lora):
    t, w = proj.shape
    tm = _tile(s, 512, 8)
    per = s // tm
    t_spec = pl.BlockSpec((1, tm, LANES), lambda i: (jnp.where(i // per >= nbp, 1, 0), i % per, 0))
    return pl.pallas_call(
        functools.partial(_mla_prep_kernel, q_lora=q_lora, kv_lora=kv_lora),
        out_shape=(jax.ShapeDtypeStruct((t, q_lora), BF16),
                   jax.ShapeDtypeStruct((t, kv_lora + LANES), F32)),
        grid=(t // tm,),
        in_specs=[pl.BlockSpec((tm, w), lambda i: (i, 0)),
                  pl.BlockSpec((1, q_lora), lambda i: (0, 0)),
                  pl.BlockSpec((1, kv_lora), lambda i: (0, 0)),
                  t_spec, t_spec],
        out_specs=(pl.BlockSpec((tm, q_lora), lambda i: (i, 0)),
                   pl.BlockSpec((tm, kv_lora + LANES), lambda i: (i, 0))),
        compiler_params=_cp("parallel"),
        name="mla_prep",
    )(proj, q_norm.reshape(1, q_lora).astype(F32), kv_norm.reshape(1, kv_lora).astype(F32), cos_t, sin_t)


def _mla_attn_kernel(q_ref, kn_ref, v_ref, kr_ref, o_ref, kf_sc, vf_sc):
    @pl.when(pl.program_id(2) == 0)
    def _():
        kf_sc[:, :HEAD_DIM] = kn_ref[...]
        kf_sc[:, HEAD_DIM:] = kr_ref[...]
        vf_sc[:, :HEAD_DIM] = v_ref[...]
        vf_sc[:, HEAD_DIM:] = jnp.ones((vf_sc.shape[0], HEAD_DIM), BF16)

    tq = q_ref.shape[0]
    sub = tq // MLA_Q_SUBBLOCKS
    ss = [_dot_nt(q_ref[i * sub:(i + 1) * sub, :], kf_sc[...]) for i in range(MLA_Q_SUBBLOCKS)]
    ps = [jnp.exp2(s - jnp.max(s, axis=1, keepdims=True)).astype(BF16) for s in ss]
    for i, p in enumerate(ps):
        ol = _dot(p, vf_sc[...])
        o_ref[i * sub:(i + 1) * sub, :] = (ol[:, :HEAD_DIM] / ol[:, HEAD_DIM:HEAD_DIM + 1]).astype(o_ref.dtype)


def mla_attention(q, kvh, kr, qrow0, n_seq, lq, kvrow0, lk, o_prev=None):
    t = q.shape[0]
    n_heads = q.shape[1] // (2 * HEAD_DIM)
    tq = _tile(lq, MLA_Q_ROWS, 8)
    nq = lq // tq
    qb0 = qrow0 // tq
    kb0 = kvrow0 // lk
    in_specs = [
        pl.BlockSpec((tq, 2 * HEAD_DIM), lambda b, h, i: (qb0 + b * nq + i, h)),
        pl.BlockSpec((lk, HEAD_DIM), lambda b, h, i: (kb0 + b, 2 * h)),
        pl.BlockSpec((lk, HEAD_DIM), lambda b, h, i: (kb0 + b, 2 * h + 1)),
        pl.BlockSpec((lk, LANES), lambda b, h, i: (kb0 + b, 0)),
    ]
    args = [q, kvh, kvh, kr]
    body = _mla_attn_kernel
    aliases = {}
    if o_prev is not None:
        in_specs.append(pl.BlockSpec(memory_space=pl.ANY))
        args.append(o_prev)
        aliases = {4: 0}

        def body(q_ref, kn_ref, v_ref, kr_ref, _prev, o_ref, kf_sc, vf_sc):
            _mla_attn_kernel(q_ref, kn_ref, v_ref, kr_ref, o_ref, kf_sc, vf_sc)

    return pl.pallas_call(
        body,
        out_shape=jax.ShapeDtypeStruct((t, n_heads * HEAD_DIM), BF16),
        grid=(n_seq, n_heads, nq),
        in_specs=in_specs,
        out_specs=pl.BlockSpec((tq, HEAD_DIM), lambda b, h, i: (qb0 + b * nq + i, h)),
        scratch_shapes=[pltpu.VMEM((lk, 2 * HEAD_DIM), BF16), pltpu.VMEM((lk, 2 * HEAD_DIM), BF16)],
        input_output_aliases=aliases,
        compiler_params=_cp("parallel", "parallel", "arbitrary"),
        name="mla_attention",
    )(*args)


def _rope_tables(s):
    rows = s // GRID_W
    r = jnp.repeat(jnp.arange(rows), GRID_W).astype(F32)
    col = jnp.tile(jnp.arange(GRID_W), rows).astype(F32)
    inv = ROPE_THETA ** (-jnp.arange(0, AXIS_ROT, 2, dtype=F32) / AXIS_ROT)
    ar, ac = r[:, None] * inv, col[:, None] * inv
    zeros = jnp.zeros((s, LANES - ROPE_DIM), F32)
    cos = jnp.concatenate([jnp.cos(ar), jnp.cos(ar), jnp.cos(ac), jnp.cos(ac), zeros], axis=1)
    sin = jnp.concatenate([-jnp.sin(ar), jnp.sin(ar), -jnp.sin(ac), jnp.sin(ac), zeros], axis=1)
    ident_c = jnp.concatenate([jnp.ones((s, ROPE_DIM), F32), zeros], axis=1)
    return jnp.stack([ident_c, cos]), jnp.stack([jnp.zeros((s, LANES), F32), sin])


def kernel(x_prompt, x_sample, state_gdn, cache_mla, c, c_ctx, ada_w, ada_b, norm1, norm2, gdn_w_in, gdn_conv, gdn_a_log, gdn_dt_bias, gdn_norm, gdn_w_out, mla_w_in, mla_q_norm, mla_w_qb, mla_kv_norm, mla_w_kvb, mla_w_out, ffn_w_gate, ffn_w_up, ffn_w_down, moe_router, moe_w_gate, moe_w_up, moe_w_down, final_norm):
    batch, seq, d = x_prompt.shape
    dec_batch, s, _ = x_sample.shape
    depth = ada_w.shape[0]
    assert (batch * seq) % s == 0 and seq % (2 * CHUNK) == 0 and s % GRID_W == 0
    nbp = batch * seq // s
    nb = nbp + dec_batch
    t = nb * s
    tp = nbp * s
    n_heads = d // HEAD_DIM
    past = cache_mla.shape[2]
    q_lora = mla_q_norm.shape[1]
    kv_lora = mla_kv_norm.shape[1]
    lk_s = past + s

    x = jnp.concatenate([x_prompt.reshape(nbp, s, d), x_sample], axis=0).reshape(t, d)

    cvec = jnp.concatenate([jnp.broadcast_to(c_ctx[None], (nbp, d)), c], axis=0)
    rpad = -nb % 8
    cvec = jnp.pad(cvec, ((0, rpad), (0, 0)))
    mod_all = adaln_all(cvec, ada_w, ada_b)[:, :nb].reshape(depth, nb, 6, d)

    cos_t, sin_t = _rope_tables(s)
    zero_state = jnp.zeros((batch, 2, n_heads, HEAD_DIM, HEAD_DIM), F32)
    new_gdn, new_mla = [], []

    for l in range(depth):
        modt = mod_all[l]
        j = l // 2
        h = norm_mod(x.reshape(nb, s, d), norm1[l], modt, 0).reshape(t, d)
        if l % 2 == 0:
            w_in = gdn_w_in[j].astype(BF16)
            qkvz = matmul(h, w_in[:, :4 * d], GDN_PROJ_DTYPE)
            ab = matmul(h, w_in[:, 4 * d:], F32)
            qkv = gdn_prep(qkvz, gdn_conv[j], nb, s, seq, nbp).reshape(t, 3 * d)
            gates, gates_c = gdn_gates(ab, gdn_a_log[j], gdn_dt_bias[j], nb, s)
            o2, st = gdn_core(qkv, gates, gates_c, zero_state, 0, batch, seq, s)
            o2, _ = gdn_core(qkv, gates, gates_c, state_gdn[:, j], tp, dec_batch, s, s, o_prev=o2)
            new_gdn.append(st)
            mix = gdn_post(o2, qkvz, gdn_norm[j])
            w_out = gdn_w_out[j].astype(BF16)
        else:
            w_in = jnp.pad(mla_w_in[j], ((0, 0), (0, LANES - ROPE_DIM))).astype(BF16)
            proj = matmul(h, w_in, F32, tm=512, tn=w_in.shape[1])
            cq, kv_ent = mla_prep(proj, mla_q_norm[j], mla_kv_norm[j], cos_t, sin_t, s, nbp, q_lora, kv_lora)
            new_mla.append(kv_ent[:tp, :kv_lora + ROPE_DIM].reshape(batch, seq, kv_lora + ROPE_DIM))
            wq = mla_w_qb[j].reshape(q_lora, n_heads, HEAD_DIM + ROPE_DIM)
            wq = jnp.pad(wq, ((0, 0), (0, 0), (0, HEAD_DIM - ROPE_DIM))).reshape(q_lora, n_heads * 2 * HEAD_DIM)
            qf = matmul_rope(cq, wq.astype(BF16), cos_t, sin_t, s, nbp)
            cache = jnp.pad(cache_mla[:, j], ((0, 0), (0, 0), (0, LANES - ROPE_DIM)))
            kv_s = jnp.concatenate([cache, kv_ent[tp:].reshape(dec_batch, s, kv_lora + LANES)], axis=1)
            kv_all = jnp.concatenate([kv_s.reshape(dec_batch * lk_s, kv_lora + LANES), kv_ent[:tp]], axis=0)
            kv_all = kv_all.astype(BF16)
            kvh = matmul(kv_all[:, :kv_lora], mla_w_kvb[j].astype(BF16), BF16)
            kr = kv_all[:, kv_lora:]
            att = mla_attention(qf, kvh, kr, 0, batch, seq, dec_batch * lk_s, seq)
            mix = mla_attention(qf, kvh, kr, tp, dec_batch, s, 0, lk_s, o_prev=att)
            w_out = mla_w_out[j].astype(BF16)
        x = matmul_resid(mix, w_out, x, modt, 2, s)

        if l % 2 == 0:
            h = norm_mod(x.reshape(nb, s, d), norm2[l], modt, 3).reshape(t, d)
            mid = matmul_swiglu(h, ffn_w_gate[j].astype(BF16), ffn_w_up[j].astype(BF16))
            w_down = ffn_w_down[j].astype(BF16)
        else:
            h, comb = norm_mod(x.reshape(nb, s, d), norm2[l], modt, 3, router=moe_router[j])
            h = h.reshape(t, d)
            n_exp, _, fe = moe_w_gate[j].shape
            wg = moe_w_gate[j].transpose(1, 0, 2).reshape(d, n_exp * fe).astype(BF16)
            wu = moe_w_up[j].transpose(1, 0, 2).reshape(d, n_exp * fe).astype(BF16)
            mid = matmul_swiglu(h, wg, wu, comb=comb.reshape(t, n_exp), cols_per_expert=fe)
            w_down = moe_w_down[j].reshape(n_exp * fe, d).astype(BF16)
        x = matmul_resid(mid, w_down, x, modt, 5, s, tm=1024 if mid.shape[1] <= 6144 else 512, tn=512)

    x3 = x.reshape(nb, s, d)
    y_prompt = final_rmsnorm(x3, final_norm, 0, nbp).reshape(batch, seq, d)
    y_sample = final_rmsnorm(x3, final_norm, nbp, dec_batch)
    return (y_prompt, y_sample, jnp.stack(new_gdn, axis=1), jnp.stack(new_mla, axis=1))
```

```python
import functools

import jax
import jax.numpy as jnp
from jax import lax
from jax.experimental import pallas as pl
from jax.experimental.pallas import tpu as pltpu

F32 = jnp.float32
BF16 = jnp.bfloat16

HEAD_DIM = 128
ROPE_DIM = 64
AXIS_ROT = ROPE_DIM // 2
ROPE_THETA = 10000.0
GRID_W = 64
CHUNK = 64
TOP_K = 2
EPS = 1e-6
MLA_SCALE = (HEAD_DIM + ROPE_DIM) ** -0.5
LANES = 128
GDN_HEADS_PER_STEP = 32
GDN_BLOCK_ROWS = 128
MLA_Q_ROWS = 1024
MLA_Q_SUBBLOCKS = 2
LOG2E = 1.4426950408889634
GDN_PROJ_DTYPE = BF16
GDN_QKV_DTYPE = BF16
GDN_O_DTYPE = BF16
SOLVE_BASE = 8
V7X_VMEM_LIMIT = 56 * 1024 * 1024


def _cp(*sem):
    return pltpu.CompilerParams(dimension_semantics=sem, vmem_limit_bytes=V7X_VMEM_LIMIT)


def _tile(dim, pref, align):
    if dim <= pref:
        return dim
    t = (pref // align) * align
    while t >= align:
        if dim % t == 0:
            return t
        t -= align
    return dim


def _dot(a, b):
    return jnp.dot(a, b, preferred_element_type=F32)


def _dot_nt(a, b):
    return lax.dot_general(a, b, (((1,), (1,)), ((), ())), preferred_element_type=F32)


def _bdot(a, b):
    return lax.dot_general(a, b, (((2,), (1,)), ((0,), (0,))), preferred_element_type=F32)


def _bdot_nt(a, b):
    return lax.dot_general(a, b, (((2,), (2,)), ((0,), (0,))), preferred_element_type=F32)


def _bdot_tn(a, b):
    return lax.dot_general(a, b, (((1,), (1,)), ((0,), (0,))), preferred_element_type=F32)


def _sigmoid(x):
    return 1.0 / (1.0 + jnp.exp(-x))


def _silu(x):
    return x * _sigmoid(x)


def _adaln_kernel(c_ref, w_ref, b_ref, o_ref):
    sc = _silu(c_ref[...]).astype(BF16)
    o_ref[0] = _dot(sc, w_ref[0].astype(BF16)) + b_ref[0]


def adaln_all(cvec, ada_w, ada_b):
    depth, d, n = ada_w.shape
    r = cvec.shape[0]
    tn = _tile(n, 512, LANES)
    return pl.pallas_call(
        _adaln_kernel,
        out_shape=jax.ShapeDtypeStruct((depth, r, n), F32),
        grid=(depth, n // tn),
        in_specs=[
            pl.BlockSpec((r, d), lambda l, j: (0, 0)),
            pl.BlockSpec((1, d, tn), lambda l, j: (l, 0, j)),
            pl.BlockSpec((1, 1, tn), lambda l, j: (l, 0, j)),
        ],
        out_specs=pl.BlockSpec((1, r, tn), lambda l, j: (l, 0, j)),
        compiler_params=_cp("parallel", "parallel"),
        name="adaln",
    )(cvec, ada_w, ada_b.reshape(depth, 1, n))


def _rms(x, g):
    var = jnp.mean(x * x, axis=-1, keepdims=True)
    return x * lax.rsqrt(var + EPS) * g


def _norm_mod_kernel(x_ref, g_ref, m_ref, o_ref, *, shift_row):
    y = _rms(x_ref[0], g_ref[...])
    shift = m_ref[0, shift_row:shift_row + 1, :]
    scale = m_ref[0, shift_row + 1:shift_row + 2, :]
    o_ref[0] = (y * (1.0 + scale) + shift).astype(o_ref.dtype)


def _norm_mod_router_kernel(x_ref, g_ref, m_ref, wr_ref, o_ref, comb_ref, *, shift_row, n_experts):
    y = _rms(x_ref[0], g_ref[...])
    shift = m_ref[0, shift_row:shift_row + 1, :]
    scale = m_ref[0, shift_row + 1:shift_row + 2, :]
    h = (y * (1.0 + scale) + shift).astype(BF16)
    o_ref[0] = h
    logits = _dot(h, wr_ref[...])
    lane = lax.broadcasted_iota(jnp.int32, logits.shape, 1)
    neg = jnp.float32(-jnp.inf)
    lg = jnp.where(lane < n_experts, logits, neg)
    m1 = jnp.max(lg, axis=1, keepdims=True)
    i1 = jnp.min(jnp.where(lg == m1, lane, LANES), axis=1, keepdims=True)
    lg2 = jnp.where(lane == i1, neg, lg)
    m2 = jnp.max(lg2, axis=1, keepdims=True)
    i2 = jnp.min(jnp.where(lg2 == m2, lane, LANES), axis=1, keepdims=True)
    e = jnp.exp(m2 - m1)
    g1 = 1.0 / (1.0 + e)
    g2 = e / (1.0 + e)
    comb = jnp.where(lane == i1, g1, 0.0) + jnp.where(lane == i2, g2, 0.0)
    comb_ref[0] = comb[:, :n_experts]


def norm_mod(x, g, modt, shift_row, router=None):
    nb, s, d = x.shape
    tm = _tile(s, 512, 8)
    x_spec = pl.BlockSpec((1, tm, d), lambda b, i: (b, i, 0))
    g_spec = pl.BlockSpec((1, d), lambda b, i: (0, 0))
    m_spec = pl.BlockSpec((1, 6, d), lambda b, i: (b, 0, 0))
    if router is None:
        return pl.pallas_call(
            functools.partial(_norm_mod_kernel, shift_row=shift_row),
            out_shape=jax.ShapeDtypeStruct((nb, s, d), BF16),
            grid=(nb, s // tm),
            in_specs=[x_spec, g_spec, m_spec],
            out_specs=x_spec,
            compiler_params=_cp("parallel", "parallel"),
            name="norm_mod",
        )(x, g.reshape(1, d), modt)
    n_experts = router.shape[1]
    wr = jnp.zeros((d, LANES), BF16).at[:, :n_experts].set(router.astype(BF16))
    return pl.pallas_call(
        functools.partial(_norm_mod_router_kernel, shift_row=shift_row, n_experts=n_experts),
        out_shape=(jax.ShapeDtypeStruct((nb, s, d), BF16),
                   jax.ShapeDtypeStruct((nb, s, n_experts), F32)),
        grid=(nb, s // tm),
        in_specs=[x_spec, g_spec, m_spec, pl.BlockSpec((d, LANES), lambda b, i: (0, 0))],
        out_specs=(x_spec, pl.BlockSpec((1, tm, n_experts), lambda b, i: (b, i, 0))),
        compiler_params=_cp("parallel", "parallel"),
        name="norm_mod_router",
    )(x, g.reshape(1, d), modt, wr)


def _final_norm_kernel(x_ref, g_ref, o_ref):
    o_ref[0] = _rms(x_ref[0], g_ref[...])


def final_rmsnorm(x, g, nb0, nbn):
    _, s, d = x.shape
    tm = _tile(s, 512, 8)
    return pl.pallas_call(
        _final_norm_kernel,
        out_shape=jax.ShapeDtypeStruct((nbn, s, d), F32),
        grid=(nbn, s // tm),
        in_specs=[pl.BlockSpec((1, tm, d), lambda b, i: (b + nb0, i, 0)),
                  pl.BlockSpec((1, d), lambda b, i: (0, 0))],
        out_specs=pl.BlockSpec((1, tm, d), lambda b, i: (b, i, 0)),
        compiler_params=_cp("parallel", "parallel"),
        name="final_norm",
    )(x, g.reshape(1, d))


def _mm_kernel(x_ref, w_ref, o_ref):
    o_ref[...] = _dot(x_ref[...], w_ref[...]).astype(o_ref.dtype)


def matmul(x, w, out_dtype, tm=1024, tn=1024):
    m, k = x.shape
    n = w.shape[1]
    tm = _tile(m, tm, 8)
    tn = _tile(n, tn, LANES)
    return pl.pallas_call(
        _mm_kernel,
        out_shape=jax.ShapeDtypeStruct((m, n), out_dtype),
        grid=(m // tm, n // tn),
        in_specs=[pl.BlockSpec((tm, k), lambda i, j: (i, 0)),
                  pl.BlockSpec((k, tn), lambda i, j: (0, j))],
        out_specs=pl.BlockSpec((tm, tn), lambda i, j: (i, j)),
        compiler_params=_cp("parallel", "parallel"),
        name="matmul",
    )(x, w)


def _mm_resid_kernel(h_ref, w_ref, x_ref, m_ref, o_ref, *, gate_row):
    gate = m_ref[0, gate_row:gate_row + 1, :]
    o_ref[...] = x_ref[...] + gate * _dot(h_ref[...], w_ref[...])


def matmul_resid(h, w, x, modt, gate_row, s, tm=1024, tn=1024):
    m, k = h.shape
    n = w.shape[1]
    tm = _tile(s, tm, 8)
    tn = _tile(n, tn, LANES)
    per = s // tm
    return pl.pallas_call(
        functools.partial(_mm_resid_kernel, gate_row=gate_row),
        out_shape=jax.ShapeDtypeStruct((m, n), F32),
        grid=(m // tm, n // tn),
        in_specs=[pl.BlockSpec((tm, k), lambda i, j: (i, 0)),
                  pl.BlockSpec((k, tn), lambda i, j: (0, j)),
                  pl.BlockSpec((tm, tn), lambda i, j: (i, j)),
                  pl.BlockSpec((1, 6, tn), lambda i, j: (i // per, 0, j))],
        out_specs=pl.BlockSpec((tm, tn), lambda i, j: (i, j)),
        input_output_aliases={2: 0},
        compiler_params=_cp("parallel", "parallel"),
        name="matmul_resid",
    )(h, w, x, modt)


def _mm_swiglu_kernel(h_ref, wg_ref, wu_ref, o_ref):
    h = h_ref[...]
    a = _dot(h, wg_ref[...])
    b = _dot(h, wu_ref[...])
    o_ref[...] = (_silu(a) * b).astype(o_ref.dtype)


def _mm_swiglu_scaled_kernel(h_ref, wg_ref, wu_ref, c_ref, o_ref, *, cols_per_expert):
    h = h_ref[...]
    a = _dot(h, wg_ref[...])
    b = _dot(h, wu_ref[...])
    tn = o_ref.shape[1]
    e = (pl.program_id(1) * tn) // cols_per_expert
    comb = c_ref[...]
    lane = lax.broadcasted_iota(jnp.int32, comb.shape, 1)
    sel = jnp.sum(jnp.where(lane == e, comb, 0.0), axis=1, keepdims=True)
    o_ref[...] = (_silu(a) * b * sel).astype(o_ref.dtype)


def matmul_swiglu(h, wg, wu, comb=None, cols_per_expert=None, tm=1024, tn=512):
    m, k = h.shape
    n = wg.shape[1]
    tm = _tile(m, tm, 8)
    tn = _tile(n if comb is None else cols_per_expert, tn, LANES)
    in_specs = [pl.BlockSpec((tm, k), lambda i, j: (i, 0)),
                pl.BlockSpec((k, tn), lambda i, j: (0, j)),
                pl.BlockSpec((k, tn), lambda i, j: (0, j))]
    args = [h, wg, wu]
    if comb is None:
        body = _mm_swiglu_kernel
    else:
        body = functools.partial(_mm_swiglu_scaled_kernel, cols_per_expert=cols_per_expert)
        in_specs.append(pl.BlockSpec((tm, comb.shape[1]), lambda i, j: (i, 0)))
        args.append(comb)
    return pl.pallas_call(
        body,
        out_shape=jax.ShapeDtypeStruct((m, n), BF16),
        grid=(m // tm, n // tn),
        in_specs=in_specs,
        out_specs=pl.BlockSpec((tm, tn), lambda i, j: (i, j)),
        compiler_params=_cp("parallel", "parallel"),
        name="matmul_swiglu",
    )(*args)


def _swap16(x):
    lane = lax.broadcasted_iota(jnp.int32, x.shape, 1)
    lo = (lane % 32) < 16
    return jnp.where(lo, pltpu.roll(x, LANES - 16, 1), pltpu.roll(x, 16, 1))


def _mm_rope_kernel(h_ref, w_ref, c_ref, s_ref, o_ref):
    acc = _dot(h_ref[...], w_ref[...]) * (MLA_SCALE * LOG2E)
    cs = c_ref[0]
    sn = s_ref[0]
    for blk in range(o_ref.shape[1] // LANES):
        x = acc[:, blk * LANES:(blk + 1) * LANES]
        if blk % 2 == 1:
            x = x * cs + _swap16(x) * sn
        o_ref[:, blk * LANES:(blk + 1) * LANES] = x.astype(o_ref.dtype)


def matmul_rope(h, w, cos_t, sin_t, s, nbp, tm=1024, tn=1024):
    m, k = h.shape
    n = w.shape[1]
    tm = _tile(s, tm, 8)
    tn = _tile(n, tn, 2 * LANES)
    per = s // tm
    t_spec = pl.BlockSpec((1, tm, LANES), lambda i, j: (jnp.where(i // per >= nbp, 1, 0), i % per, 0))
    return pl.pallas_call(
        _mm_rope_kernel,
        out_shape=jax.ShapeDtypeStruct((m, n), BF16),
        grid=(m // tm, n // tn),
        in_specs=[pl.BlockSpec((tm, k), lambda i, j: (i, 0)),
                  pl.BlockSpec((k, tn), lambda i, j: (0, j)),
                  t_spec, t_spec],
        out_specs=pl.BlockSpec((tm, tn), lambda i, j: (i, j)),
        compiler_params=_cp("parallel", "parallel"),
        name="matmul_rope",
    )(h, w, cos_t, sin_t)


def _gdn_prep_kernel(cur_ref, prev_ref, next_ref, w_ref, o_ref, pad_ref, *, seq_p, seq_s, nbp, d_model):
    b = pl.program_id(0)
    i = pl.program_id(1)
    j = pl.program_id(2)
    tm, tc = cur_ref.shape[1], cur_ref.shape[2]
    k_taps = w_ref.shape[0]
    half = k_taps // 2
    seqlen = jnp.where(b < nbp, seq_p, seq_s)
    row0 = i * tm
    at_start = (row0 % seqlen) == 0
    at_end = ((row0 + tm) % seqlen) == 0
    pad_ref[8:8 + tm, :] = cur_ref[0].astype(F32)
    pad_ref[0:8, :] = jnp.where(at_start, 0.0, prev_ref[0].astype(F32))
    pad_ref[8 + tm:16 + tm, :] = jnp.where(at_end, 0.0, next_ref[0].astype(F32))
    w = w_ref[...]
    acc = jnp.zeros((tm, tc), F32)
    for t in range(k_taps):
        acc = acc + pad_ref[8 - half + t:8 - half + t + tm, :] * w[t:t + 1, :]
    y = _silu(acc)
    col0 = j * tc
    is_qk = col0 < 2 * d_model
    qscale = jnp.where(col0 < d_model, HEAD_DIM ** -0.5, 1.0)
    for hb in range(tc // HEAD_DIM):
        yh = y[:, hb * HEAD_DIM:(hb + 1) * HEAD_DIM]
        ssq = jnp.sum(yh * yh, axis=1, keepdims=True)
        fac = jnp.where(is_qk, lax.rsqrt(ssq + EPS) * qscale, 1.0)
        o_ref[0, :, hb * HEAD_DIM:(hb + 1) * HEAD_DIM] = (yh * fac).astype(o_ref.dtype)


def gdn_prep(qkvz, conv_w, nb, s, seq_p, nbp):
    t, d4 = qkvz.shape
    d = d4 // 4
    x3 = qkvz.reshape(nb, s, d4)
    tm = _tile(min(s, seq_p), 256, 8)
    tc = _tile(d, 512, HEAD_DIM)
    r8 = tm // 8
    last8 = s // 8 - 1
    return pl.pallas_call(
        functools.partial(_gdn_prep_kernel, seq_p=seq_p, seq_s=s, nbp=nbp, d_model=d),
        out_shape=jax.ShapeDtypeStruct((nb, s, 3 * d), GDN_QKV_DTYPE),
        grid=(nb, s // tm, 3 * d // tc),
        in_specs=[
            pl.BlockSpec((1, tm, tc), lambda b, i, j: (b, i, j)),
            pl.BlockSpec((1, 8, tc), lambda b, i, j: (b, jnp.maximum(i * r8 - 1, 0), j)),
            pl.BlockSpec((1, 8, tc), lambda b, i, j: (b, jnp.minimum((i + 1) * r8, last8), j)),
            pl.BlockSpec((conv_w.shape[0], tc), lambda b, i, j: (0, j)),
        ],
        out_specs=pl.BlockSpec((1, tm, tc), lambda b, i, j: (b, i, j)),
        scratch_shapes=[pltpu.VMEM((tm + 16, tc), F32)],
        compiler_params=_cp("parallel", "parallel", "parallel"),
        name="gdn_prep",
    )(x3, x3, x3, conv_w)


def _split3(x):
    x1 = x.astype(BF16)
    r = x - x1.astype(F32)
    x2 = r.astype(BF16)
    x3 = (r - x2.astype(F32)).astype(BF16)
    return x1, x2, x3


def _gdn_gates_kernel(ab_ref, alog_ref, dt_ref, o_ref, oc_ref, *, n_heads):
    x = ab_ref[0]
    tm = x.shape[0]
    lane = lax.broadcasted_iota(jnp.int32, x.shape, 1)
    xa = x + dt_ref[...]
    sp = jnp.maximum(xa, 0.0) + jnp.log1p(jnp.exp(-jnp.abs(xa)))
    g = -jnp.exp(alog_ref[...]) * sp
    beta = _sigmoid(x)
    ri = lax.broadcasted_iota(jnp.int32, (tm, tm), 0)
    ci = lax.broadcasted_iota(jnp.int32, (tm, tm), 1)
    same = (ri // CHUNK) == (ci // CHUNK)
    m_pre = jnp.where(same & (ci <= ri), 1.0, 0.0).astype(BF16)
    m_suf = jnp.where(same & (ci >= ri), 1.0, 0.0).astype(BF16)
    pre = jnp.zeros_like(x)
    suf = jnp.zeros_like(x)
    for part in _split3(g):
        pre = pre + _dot(m_pre, part)
        suf = suf + _dot(m_suf, part)
    res = jnp.where(lane < n_heads, pre, jnp.where(lane < 2 * n_heads, suf, beta))
    o_ref[0] = res.T
    oc_ref[0] = res


def gdn_gates(ab, a_log, dt_bias, nb, s):
    t, w = ab.shape
    n_heads = w // 4
    tm = _tile(s, 256, CHUNK)
    pad = jnp.zeros((1, w - 2 * n_heads), F32)
    alog = jnp.concatenate([a_log.reshape(1, 2 * n_heads).astype(F32), pad], axis=1)
    dtb = jnp.concatenate([dt_bias.reshape(1, 2 * n_heads).astype(F32), pad], axis=1)
    return pl.pallas_call(
        functools.partial(_gdn_gates_kernel, n_heads=n_heads),
        out_shape=(jax.ShapeDtypeStruct((nb, w, s), F32),
                   jax.ShapeDtypeStruct((nb, s, w), F32)),
        grid=(nb, s // tm),
        in_specs=[pl.BlockSpec((1, tm, w), lambda b, i: (b, i, 0)),
                  pl.BlockSpec((1, w), lambda b, i: (0, 0)),
                  pl.BlockSpec((1, w), lambda b, i: (0, 0))],
        out_specs=(pl.BlockSpec((1, w, tm), lambda b, i: (b, 0, i)),
                   pl.BlockSpec((1, tm, w), lambda b, i: (b, i, 0))),
        compiler_params=_cp("parallel", "parallel"),
        name="gdn_gates",
    )(ab.reshape(nb, s, w), alog, dtb)


def _gdn_core_kernel(q_ref, k_ref, v_ref, g_ref, gc_ref, s0_ref, o_ref, sfin_ref,
                     state_sc, wq_sc, u_sc, kd_sc, intra_sc, gl_sc, *, hg, nc, n_heads):
    dirn = pl.program_id(0)
    blk = pl.program_id(3)
    nblk = pl.num_programs(3)
    rev = dirn == 1
    c = CHUNK

    @pl.when(blk == 0)
    def _():
        state_sc[...] = s0_ref[0, 0]

    ii = lax.broadcasted_iota(jnp.int32, (c, c), 0)
    jj = lax.broadcasted_iota(jnp.int32, (c, c), 1)
    dmat = jnp.where(rev, jj - ii, ii - jj)
    incl = dmat >= 0
    strict = dmat > 0

    rows = g_ref[0, 0, 0]
    w4 = 4 * n_heads
    lane0 = dirn * n_heads + pl.program_id(2) * hg

    qs, ks, vs, grows = [], [], [], []
    c_g, c_b, c_eg, c_beg, c_kdm, c_gl = [], [], [], [], [], []
    for cc in range(nc):
        r0 = cc * c
        gt = gc_ref[0, r0:r0 + c, :]
        gg = pltpu.roll(gt, (w4 - lane0) % w4, 1)
        bb = pltpu.roll(gt, 2 * n_heads - lane0, 1)
        eg = jnp.exp(gg)
        beg = bb * eg
        glast = jnp.where(rev, gg[0:1], gg[c - 1:c])
        kdm = jnp.exp(glast - gg)
        glt = jnp.exp(glast)
        for hh in range(hg):
            cs = slice(hh * HEAD_DIM, (hh + 1) * HEAD_DIM)
            sl = slice(hh, hh + 1)
            qs.append(q_ref[r0:r0 + c, cs])
            ks.append(k_ref[r0:r0 + c, cs])
            vs.append(v_ref[r0:r0 + c, cs])
            grows.append(rows[hh:hh + 1, r0:r0 + c])
            c_g.append(gg[:, sl])
            c_b.append(bb[:, sl])
            c_eg.append(eg[:, sl])
            c_beg.append(beg[:, sl])
            c_kdm.append(kdm[:, sl])
            c_gl.append(glt[:, sl])
    q = jnp.stack(qs)
    k = jnp.stack(ks)
    qf = q.astype(F32)
    kf = k.astype(F32)
    vf = jnp.stack(vs).astype(F32)
    gcol = jnp.stack(c_g)
    bcol = jnp.stack(c_b)
    grow = jnp.stack(grows)
    decay = jnp.where(incl, jnp.exp(jnp.where(incl, gcol - grow, 0.0)), 0.0)
    kb = k.astype(BF16)
    qkk = _bdot_nt(jnp.concatenate([q.astype(BF16), kb], axis=1), kb)
    intra = qkk[:, :c] * decay
    nmat = jnp.where(strict, qkk[:, c:] * bcol * decay, 0.0)
    def bb(x):
        return x.astype(BF16)

    blk_i = ii // SOLVE_BASE
    blk_j = jj // SOLVE_BASE
    m0 = jnp.where(blk_i == blk_j, -nmat, 0.0)
    q1 = _bdot(bb(m0), bb(m0))
    q1b = bb(q1)
    a1 = m0 + q1 + _bdot(bb(m0), q1b)
    q2 = _bdot(q1b, q1b)
    r = a1 + q2 + _bdot(bb(a1), bb(q2))
    width = SOLVE_BASE
    while width < c:
        pair = (ii // (2 * width)) == (jj // (2 * width))
        noff = jnp.where(pair & ((ii // width) != (jj // width)), nmat, 0.0)
        y = noff + _bdot(bb(r), bb(noff))
        r = r - (y + _bdot(bb(y), bb(r)))
        width *= 2
    rhs = jnp.concatenate([vf * bcol, kf * jnp.stack(c_beg)], axis=2)
    sol = rhs + _bdot(r.astype(BF16), rhs.astype(BF16))
    u_sc[...] = sol[:, :, :HEAD_DIM]
    wq_sc[:, 0:c, :] = sol[:, :, HEAD_DIM:].astype(BF16)
    wq_sc[:, c:2 * c, :] = (qf * jnp.stack(c_eg)).astype(BF16)
    kd_sc[...] = (kf * jnp.stack(c_kdm)).astype(BF16)
    intra_sc[...] = intra.astype(BF16)
    gl_sc[...] = jnp.broadcast_to(jnp.stack(c_gl), gl_sc.shape)

    for step in range(nc):
        cc = jnp.where(rev, nc - 1 - step, step)
        r0 = pl.multiple_of(cc * c, c)
        ps = pl.ds(cc * hg, hg)
        st = state_sc[...]
        ws = _bdot(wq_sc[ps], st.astype(BF16))
        v_new = u_sc[ps] - ws[:, :c]
        vb = v_new.astype(BF16)
        o = ws[:, c:] + _bdot(intra_sc[ps], vb)
        state_sc[...] = st * gl_sc[ps] + _bdot_tn(kd_sc[ps], vb)
        for hh in range(hg):
            o_ref[0, pl.ds(r0, c), hh * HEAD_DIM:(hh + 1) * HEAD_DIM] = o[hh].astype(o_ref.dtype)

    @pl.when(blk == nblk - 1)
    def _():
        sfin_ref[0, 0] = state_sc[...]


def gdn_core(qkv, gates, gates_c, s0, row0, n_seq, seq_len, s, o_prev=None):
    t, d3 = qkv.shape
    d = d3 // 3
    n_heads = d // HEAD_DIM
    hg = GDN_HEADS_PER_STEP if n_heads % GDN_HEADS_PER_STEP == 0 else n_heads
    ngrp = n_heads // hg
    lblk = _tile(seq_len, GDN_BLOCK_ROWS, 2 * CHUNK)
    nc = lblk // CHUNK
    nblk = seq_len // lblk
    wcols = hg * HEAD_DIM
    kofs = d // wcols
    rb0 = row0 // lblk
    per_s = s // lblk
    g5 = gates.reshape(gates.shape[0], 4, ngrp, hg, s)

    def eff(dirn, bk):
        return jnp.where(dirn == 1, nblk - 1 - bk, bk)

    def row_blk(dirn, q, bk):
        return rb0 + q * nblk + eff(dirn, bk)

    def qkv_spec(sec):
        return pl.BlockSpec((lblk, wcols), lambda dr, q, g, bk: (row_blk(dr, q, bk), sec * kofs + g))

    def grow_map(dr, q, g, bk):
        rb = row_blk(dr, q, bk)
        return (rb // per_s, dr, g, 0, rb % per_s)

    def gcol_map(dr, q, g, bk):
        rb = row_blk(dr, q, bk)
        return (rb // per_s, rb % per_s, 0)

    grow_spec = pl.BlockSpec((1, 1, 1, hg, lblk), grow_map)
    gcol_spec = pl.BlockSpec((1, lblk, 4 * n_heads), gcol_map)
    st_spec = pl.BlockSpec((1, 1, hg, HEAD_DIM, HEAD_DIM), lambda dr, q, g, bk: (q, dr, g, 0, 0))
    o_spec = pl.BlockSpec((1, lblk, wcols), lambda dr, q, g, bk: (dr, row_blk(dr, q, bk), g))
    in_specs = [qkv_spec(0), qkv_spec(1), qkv_spec(2), grow_spec, gcol_spec, st_spec]
    args = [qkv, qkv, qkv, g5, gates_c, s0]
    aliases = {}
    body = functools.partial(_gdn_core_kernel, hg=hg, nc=nc, n_heads=n_heads)
    if o_prev is not None:
        in_specs.append(pl.BlockSpec(memory_space=pl.ANY))
        args.append(o_prev)
        aliases = {6: 0}
        inner = body

        def body(q_ref, k_ref, v_ref, g_ref, b_ref, s0_ref, _prev, *rest):
            inner(q_ref, k_ref, v_ref, g_ref, b_ref, s0_ref, *rest)

    return pl.pallas_call(
        body,
        out_shape=(jax.ShapeDtypeStruct((2, t, d), GDN_O_DTYPE),
                   jax.ShapeDtypeStruct((n_seq, 2, n_heads, HEAD_DIM, HEAD_DIM), F32)),
        grid=(2, n_seq, ngrp, nblk),
        in_specs=in_specs,
        out_specs=(o_spec, st_spec),
        scratch_shapes=[
            pltpu.VMEM((hg, HEAD_DIM, HEAD_DIM), F32),
            pltpu.VMEM((nc * hg, 2 * CHUNK, HEAD_DIM), BF16),
            pltpu.VMEM((nc * hg, CHUNK, HEAD_DIM), F32),
            pltpu.VMEM((nc * hg, CHUNK, HEAD_DIM), BF16),
            pltpu.VMEM((nc * hg, CHUNK, CHUNK), BF16),
            pltpu.VMEM((nc * hg, 1, HEAD_DIM), F32),
        ],
        input_output_aliases=aliases,
        compiler_params=_cp("parallel", "parallel", "parallel", "arbitrary"),
        name="gdn_core",
    )(*args)


def _gdn_post_kernel(of_ref, ob_ref, z_ref, nw_ref, o_ref):
    o = of_ref[0].astype(F32) + ob_ref[0].astype(F32)
    z = z_ref[...].astype(F32)
    nw = nw_ref[...]
    for hb in range(o.shape[1] // HEAD_DIM):
        cs = slice(hb * HEAD_DIM, (hb + 1) * HEAD_DIM)
        oh = o[:, cs]
        var = jnp.mean(oh * oh, axis=1, keepdims=True)
        y = oh * lax.rsqrt(var + EPS) * nw
        o_ref[:, cs] = (y * _silu(z[:, cs])).astype(o_ref.dtype)


def gdn_post(o2, qkvz, norm_w):
    _, t, d = o2.shape
    tm = _tile(t, 512, 8)
    tc = _tile(d, 1024, HEAD_DIM)
    zofs = 3 * d // tc
    return pl.pallas_call(
        _gdn_post_kernel,
        out_shape=jax.ShapeDtypeStruct((t, d), BF16),
        grid=(t // tm, d // tc),
        in_specs=[pl.BlockSpec((1, tm, tc), lambda i, j: (0, i, j)),
                  pl.BlockSpec((1, tm, tc), lambda i, j: (1, i, j)),
                  pl.BlockSpec((tm, tc), lambda i, j: (i, zofs + j)),
                  pl.BlockSpec((1, HEAD_DIM), lambda i, j: (0, 0))],
        out_specs=pl.BlockSpec((tm, tc), lambda i, j: (i, j)),
        compiler_params=_cp("parallel", "parallel"),
        name="gdn_post",
    )(o2, o2, qkvz, norm_w.reshape(1, HEAD_DIM).astype(F32))


def _mla_prep_kernel(p_ref, qn_ref, kvn_ref, c_ref, s_ref, cq_ref, kv_ref, *, q_lora, kv_lora):
    p = p_ref[...]
    cq_ref[...] = _rms(p[:, :q_lora], qn_ref[...]).astype(cq_ref.dtype)
    kv_ref[:, :kv_lora] = _rms(p[:, q_lora:q_lora + kv_lora], kvn_ref[...])
    kr = p[:, q_lora + kv_lora:]
    kv_ref[:, kv_lora:] = kr * c_ref[0] + _swap16(kr) * s_ref[0]


def mla_prep(proj, q_norm, kv_norm, cos_t, sin_t, s, nbp, q_lora, kv_lora):
    t, w = proj.shape
    tm = _tile(s, 512, 8)
    per = s // tm
    t_spec = pl.BlockSpec((1, tm, LANES), lambda i: (jnp.where(i // per >= nbp, 1, 0), i % per, 0))
    return pl.pallas_call(
        functools.partial(_mla_prep_kernel, q_lora=q_lora, kv_lora=kv_lora),
        out_shape=(jax.ShapeDtypeStruct((t, q_lora), BF16),
                   jax.ShapeDtypeStruct((t, kv_lora + LANES), F32)),
        grid=(t // tm,),
        in_specs=[pl.BlockSpec((tm, w), lambda i: (i, 0)),
                  pl.BlockSpec((1, q_lora), lambda i: (0, 0)),
                  pl.BlockSpec((1, kv_lora), lambda i: (0, 0)),
                  t_spec, t_spec],
        out_specs=(pl.BlockSpec((tm, q_lora), lambda i: (i, 0)),
                   pl.BlockSpec((tm, kv_lora + LANES), lambda i: (i, 0))),
        compiler_params=_cp("parallel"),
        name="mla_prep",
    )(proj, q_norm.reshape(1, q_lora).astype(F32), kv_norm.reshape(1, kv_lora).astype(F32), cos_t, sin_t)


def _mla_attn_kernel(q_ref, kn_ref, v_ref, kr_ref, o_ref, kf_sc, vf_sc):
    @pl.when(pl.program_id(2) == 0)
    def _():
        kf_sc[:, :HEAD_DIM] = kn_ref[...]
        kf_sc[:, HEAD_DIM:] = kr_ref[...]
        vf_sc[:, :HEAD_DIM] = v_ref[...]
        vf_sc[:, HEAD_DIM:] = jnp.ones((vf_sc.shape[0], HEAD_DIM), BF16)

    tq = q_ref.shape[0]
    sub = tq // MLA_Q_SUBBLOCKS
    ss = [_dot_nt(q_ref[i * sub:(i + 1) * sub, :], kf_sc[...]) for i in range(MLA_Q_SUBBLOCKS)]
    ps = [jnp.exp2(s - jnp.max(s, axis=1, keepdims=True)).astype(BF16) for s in ss]
    for i, p in enumerate(ps):
        ol = _dot(p, vf_sc[...])
        o_ref[i * sub:(i + 1) * sub, :] = (ol[:, :HEAD_DIM] / ol[:, HEAD_DIM:HEAD_DIM + 1]).astype(o_ref.dtype)


def mla_attention(q, kvh, kr, qrow0, n_seq, lq, kvrow0, lk, o_prev=None):
    t = q.shape[0]
    n_heads = q.shape[1] // (2 * HEAD_DIM)
    tq = _tile(lq, MLA_Q_ROWS, 8)
    nq = lq // tq
    qb0 = qrow0 // tq
    kb0 = kvrow0 // lk
    in_specs = [
        pl.BlockSpec((tq, 2 * HEAD_DIM), lambda b, h, i: (qb0 + b * nq + i, h)),
        pl.BlockSpec((lk, HEAD_DIM), lambda b, h, i: (kb0 + b, 2 * h)),
        pl.BlockSpec((lk, HEAD_DIM), lambda b, h, i: (kb0 + b, 2 * h + 1)),
        pl.BlockSpec((lk, LANES), lambda b, h, i: (kb0 + b, 0)),
    ]
    args = [q, kvh, kvh, kr]
    body = _mla_attn_kernel
    aliases = {}
    if o_prev is not None:
        in_specs.append(pl.BlockSpec(memory_space=pl.ANY))
        args.append(o_prev)
        aliases = {4: 0}

        def body(q_ref, kn_ref, v_ref, kr_ref, _prev, o_ref, kf_sc, vf_sc):
            _mla_attn_kernel(q_ref, kn_ref, v_ref, kr_ref, o_ref, kf_sc, vf_sc)

    return pl.pallas_call(
        body,
        out_shape=jax.ShapeDtypeStruct((t, n_heads * HEAD_DIM), BF16),
        grid=(n_seq, n_heads, nq),
        in_specs=in_specs,
        out_specs=pl.BlockSpec((tq, HEAD_DIM), lambda b, h, i: (qb0 + b * nq + i, h)),
        scratch_shapes=[pltpu.VMEM((lk, 2 * HEAD_DIM), BF16), pltpu.VMEM((lk, 2 * HEAD_DIM), BF16)],
        input_output_aliases=aliases,
        compiler_params=_cp("parallel", "parallel", "arbitrary"),
        name="mla_attention",
    )(*args)


def _rope_tables(s):
    rows = s // GRID_W
    r = jnp.repeat(jnp.arange(rows), GRID_W).astype(F32)
    col = jnp.tile(jnp.arange(GRID_W), rows).astype(F32)
    inv = ROPE_THETA ** (-jnp.arange(0, AXIS_ROT, 2, dtype=F32) / AXIS_ROT)
    ar, ac = r[:, None] * inv, col[:, None] * inv
    zeros = jnp.zeros((s, LANES - ROPE_DIM), F32)
    cos = jnp.concatenate([jnp.cos(ar), jnp.cos(ar), jnp.cos(ac), jnp.cos(ac), zeros], axis=1)
    sin = jnp.concatenate([-jnp.sin(ar), jnp.sin(ar), -jnp.sin(ac), jnp.sin(ac), zeros], axis=1)
    ident_c = jnp.concatenate([jnp.ones((s, ROPE_DIM), F32), zeros], axis=1)
    return jnp.stack([ident_c, cos]), jnp.stack([jnp.zeros((s, LANES), F32), sin])


def kernel(x_prompt, x_sample, state_gdn, cache_mla, c, c_ctx, ada_w, ada_b, norm1, norm2, gdn_w_in, gdn_conv, gdn_a_log, gdn_dt_bias, gdn_norm, gdn_w_out, mla_w_in, mla_q_norm, mla_w_qb, mla_kv_norm, mla_w_kvb, mla_w_out, ffn_w_gate, ffn_w_up, ffn_w_down, moe_router, moe_w_gate, moe_w_up, moe_w_down, final_norm):
    batch, seq, d = x_prompt.shape
    dec_batch, s, _ = x_sample.shape
    depth = ada_w.shape[0]
    assert (batch * seq) % s == 0 and seq % (2 * CHUNK) == 0 and s % GRID_W == 0
    nbp = batch * seq // s
    nb = nbp + dec_batch
    t = nb * s
    tp = nbp * s
    n_heads = d // HEAD_DIM
    past = cache_mla.shape[2]
    q_lora = mla_q_norm.shape[1]
    kv_lora = mla_kv_norm.shape[1]
    lk_s = past + s

    x = jnp.concatenate([x_prompt.reshape(nbp, s, d), x_sample], axis=0).reshape(t, d)

    cvec = jnp.concatenate([jnp.broadcast_to(c_ctx[None], (nbp, d)), c], axis=0)
    rpad = -nb % 8
    cvec = jnp.pad(cvec, ((0, rpad), (0, 0)))
    mod_all = adaln_all(cvec, ada_w, ada_b)[:, :nb].reshape(depth, nb, 6, d)

    cos_t, sin_t = _rope_tables(s)
    zero_state = jnp.zeros((batch, 2, n_heads, HEAD_DIM, HEAD_DIM), F32)
    new_gdn, new_mla = [], []

    for l in range(depth):
        modt = mod_all[l]
        j = l // 2
        h = norm_mod(x.reshape(nb, s, d), norm1[l], modt, 0).reshape(t, d)
        if l % 2 == 0:
            w_in = gdn_w_in[j].astype(BF16)
            qkvz = matmul(h, w_in[:, :4 * d], GDN_PROJ_DTYPE)
            ab = matmul(h, w_in[:, 4 * d:], F32)
            qkv = gdn_prep(qkvz, gdn_conv[j], nb, s, seq, nbp).reshape(t, 3 * d)
            gates, gates_c = gdn_gates(ab, gdn_a_log[j], gdn_dt_bias[j], nb, s)
            o2, st = gdn_core(qkv, gates, gates_c, zero_state, 0, batch, seq, s)
            o2, _ = gdn_core(qkv, gates, gates_c, state_gdn[:, j], tp, dec_batch, s, s, o_prev=o2)
            new_gdn.append(st)
            mix = gdn_post(o2, qkvz, gdn_norm[j])
            w_out = gdn_w_out[j].astype(BF16)
        else:
            w_in = jnp.pad(mla_w_in[j], ((0, 0), (0, LANES - ROPE_DIM))).astype(BF16)
            proj = matmul(h, w_in, F32, tm=512, tn=w_in.shape[1])
            cq, kv_ent = mla_prep(proj, mla_q_norm[j], mla_kv_norm[j], cos_t, sin_t, s, nbp, q_lora, kv_lora)
            new_mla.append(kv_ent[:tp, :kv_lora + ROPE_DIM].reshape(batch, seq, kv_lora + ROPE_DIM))
            wq = mla_w_qb[j].reshape(q_lora, n_heads, HEAD_DIM + ROPE_DIM)
            wq = jnp.pad(wq, ((0, 0), (0, 0), (0, HEAD_DIM - ROPE_DIM))).reshape(q_lora, n_heads * 2 * HEAD_DIM)
            qf = matmul_rope(cq, wq.astype(BF16), cos_t, sin_t, s, nbp)
            cache = jnp.pad(cache_mla[:, j], ((0, 0), (0, 0), (0, LANES - ROPE_DIM)))
            kv_s = jnp.concatenate([cache, kv_ent[tp:].reshape(dec_batch, s, kv_lora + LANES)], axis=1)
            kv_all = jnp.concatenate([kv_s.reshape(dec_batch * lk_s, kv_lora + LANES), kv_ent[:tp]], axis=0)
            kv_all = kv_all.astype(BF16)
            kvh = matmul(kv_all[:, :kv_lora], mla_w_kvb[j].astype(BF16), BF16)
            kr = kv_all[:, kv_lora:]
            att = mla_attention(qf, kvh, kr, 0, batch, seq, dec_batch * lk_s, seq)
            mix = mla_attention(qf, kvh, kr, tp, dec_batch, s, 0, lk_s, o_prev=att)
            w_out = mla_w_out[j].astype(BF16)
        x = matmul_resid(mix, w_out, x, modt, 2, s)

        if l % 2 == 0:
            h = norm_mod(x.reshape(nb, s, d), norm2[l], modt, 3).reshape(t, d)
            mid = matmul_swiglu(h, ffn_w_gate[j].astype(BF16), ffn_w_up[j].astype(BF16))
            w_down = ffn_w_down[j].astype(BF16)
        else:
            h, comb = norm_mod(x.reshape(nb, s, d), norm2[l], modt, 3, router=moe_router[j])
            h = h.reshape(t, d)
            n_exp, _, fe = moe_w_gate[j].shape
            wg = moe_w_gate[j].transpose(1, 0, 2).reshape(d, n_exp * fe).astype(BF16)
            wu = moe_w_up[j].transpose(1, 0, 2).reshape(d, n_exp * fe).astype(BF16)
            mid = matmul_swiglu(h, wg, wu, comb=comb.reshape(t, n_exp), cols_per_expert=fe)
            w_down = moe_w_down[j].reshape(n_exp * fe, d).astype(BF16)
        x = matmul_resid(mid, w_down, x, modt, 5, s, tm=1024 if mid.shape[1] <= 6144 else 512, tn=512)

    x3 = x.reshape(nb, s, d)
    y_prompt = final_rmsnorm(x3, final_norm, 0, nbp).reshape(batch, seq, d)
    y_sample = final_rmsnorm(x3, final_norm, nbp, dec_batch)
    return (y_prompt, y_sample, jnp.stack(new_gdn, axis=1), jnp.stack(new_mla, axis=1))
```

```python
import functools

import jax
import jax.numpy as jnp
from jax import lax
from jax.experimental import pallas as pl
from jax.experimental.pallas import tpu as pltpu

F32 = jnp.float32
BF16 = jnp.bfloat16

HEAD_DIM = 128
ROPE_DIM = 64
AXIS_ROT = ROPE_DIM // 2
ROPE_THETA = 10000.0
GRID_W = 64
CHUNK = 64
TOP_K = 2
EPS = 1e-6
MLA_SCALE = (HEAD_DIM + ROPE_DIM) ** -0.5
LANES = 128
GDN_HEADS_PER_STEP = 32
GDN_BLOCK_ROWS = 128
MLA_Q_ROWS = 1024
MLA_Q_SUBBLOCKS = 2
MLA_HEAD_KEYS = 2048
LOG2E = 1.4426950408889634
GDN_PROJ_DTYPE = BF16
GDN_QKV_DTYPE = BF16
GDN_O_DTYPE = BF16
SOLVE_BASE = 8
V7X_VMEM_LIMIT = 56 * 1024 * 1024


def _cp(*sem):
    return pltpu.CompilerParams(dimension_semantics=sem, vmem_limit_bytes=V7X_VMEM_LIMIT)


def _tile(dim, pref, align):
    if dim <= pref:
        return dim
    t = (pref // align) * align
    while t >= align:
        if dim % t == 0:
            return t
        t -= align
    return dim


def _dot(a, b):
    return jnp.dot(a, b, preferred_element_type=F32)


def _dot_nt(a, b):
    return lax.dot_general(a, b, (((1,), (1,)), ((), ())), preferred_element_type=F32)


def _bdot(a, b):
    return lax.dot_general(a, b, (((2,), (1,)), ((0,), (0,))), preferred_element_type=F32)


def _bdot_nt(a, b):
    return lax.dot_general(a, b, (((2,), (2,)), ((0,), (0,))), preferred_element_type=F32)


def _bdot_tn(a, b):
    return lax.dot_general(a, b, (((1,), (1,)), ((0,), (0,))), preferred_element_type=F32)


def _sigmoid(x):
    return 1.0 / (1.0 + jnp.exp(-x))


def _silu(x):
    return x * _sigmoid(x)


def _adaln_kernel(c_ref, w_ref, b_ref, o_ref):
    sc = _silu(c_ref[...]).astype(BF16)
    o_ref[0] = _dot(sc, w_ref[0].astype(BF16)) + b_ref[0]


def adaln_all(cvec, ada_w, ada_b):
    depth, d, n = ada_w.shape
    r = cvec.shape[0]
    tn = _tile(n, 512, LANES)
    return pl.pallas_call(
        _adaln_kernel,
        out_shape=jax.ShapeDtypeStruct((depth, r, n), F32),
        grid=(depth, n // tn),
        in_specs=[
            pl.BlockSpec((r, d), lambda l, j: (0, 0)),
            pl.BlockSpec((1, d, tn), lambda l, j: (l, 0, j)),
            pl.BlockSpec((1, 1, tn), lambda l, j: (l, 0, j)),
        ],
        out_specs=pl.BlockSpec((1, r, tn), lambda l, j: (l, 0, j)),
        compiler_params=_cp("parallel", "parallel"),
        name="adaln",
    )(cvec, ada_w, ada_b.reshape(depth, 1, n))


def _rms(x, g):
    var = jnp.mean(x * x, axis=-1, keepdims=True)
    return x * lax.rsqrt(var + EPS) * g


def _norm_mod_kernel(x_ref, g_ref, m_ref, o_ref, *, shift_row):
    y = _rms(x_ref[0], g_ref[...])
    shift = m_ref[0, shift_row:shift_row + 1, :]
    scale = m_ref[0, shift_row + 1:shift_row + 2, :]
    o_ref[0] = (y * (1.0 + scale) + shift).astype(o_ref.dtype)


def _norm_mod_router_kernel(x_ref, g_ref, m_ref, wr_ref, o_ref, comb_ref, *, shift_row, n_experts):
    y = _rms(x_ref[0], g_ref[...])
    shift = m_ref[0, shift_row:shift_row + 1, :]
    scale = m_ref[0, shift_row + 1:shift_row + 2, :]
    h = (y * (1.0 + scale) + shift).astype(BF16)
    o_ref[0] = h
    logits = _dot(h, wr_ref[...])
    lane = lax.broadcasted_iota(jnp.int32, logits.shape, 1)
    neg = jnp.float32(-jnp.inf)
    lg = jnp.where(lane < n_experts, logits, neg)
    m1 = jnp.max(lg, axis=1, keepdims=True)
    i1 = jnp.min(jnp.where(lg == m1, lane, LANES), axis=1, keepdims=True)
    lg2 = jnp.where(lane == i1, neg, lg)
    m2 = jnp.max(lg2, axis=1, keepdims=True)
    i2 = jnp.min(jnp.where(lg2 == m2, lane, LANES), axis=1, keepdims=True)
    e = jnp.exp(m2 - m1)
    g1 = 1.0 / (1.0 + e)
    g2 = e / (1.0 + e)
    comb = jnp.where(lane == i1, g1, 0.0) + jnp.where(lane == i2, g2, 0.0)
    comb_ref[0] = comb[:, :n_experts]


def norm_mod(x, g, modt, shift_row, router=None):
    nb, s, d = x.shape
    tm = _tile(s, 512, 8)
    x_spec = pl.BlockSpec((1, tm, d), lambda b, i: (b, i, 0))
    g_spec = pl.BlockSpec((1, d), lambda b, i: (0, 0))
    m_spec = pl.BlockSpec((1, 6, d), lambda b, i: (b, 0, 0))
    if router is None:
        return pl.pallas_call(
            functools.partial(_norm_mod_kernel, shift_row=shift_row),
            out_shape=jax.ShapeDtypeStruct((nb, s, d), BF16),
            grid=(nb, s // tm),
            in_specs=[x_spec, g_spec, m_spec],
            out_specs=x_spec,
            compiler_params=_cp("parallel", "parallel"),
            name="norm_mod",
        )(x, g.reshape(1, d), modt)
    n_experts = router.shape[1]
    wr = jnp.zeros((d, LANES), BF16).at[:, :n_experts].set(router.astype(BF16))
    return pl.pallas_call(
        functools.partial(_norm_mod_router_kernel, shift_row=shift_row, n_experts=n_experts),
        out_shape=(jax.ShapeDtypeStruct((nb, s, d), BF16),
                   jax.ShapeDtypeStruct((nb, s, n_experts), F32)),
        grid=(nb, s // tm),
        in_specs=[x_spec, g_spec, m_spec, pl.BlockSpec((d, LANES), lambda b, i: (0, 0))],
        out_specs=(x_spec, pl.BlockSpec((1, tm, n_experts), lambda b, i: (b, i, 0))),
        compiler_params=_cp("parallel", "parallel"),
        name="norm_mod_router",
    )(x, g.reshape(1, d), modt, wr)


def _final_norm_kernel(x_ref, g_ref, o_ref):
    o_ref[0] = _rms(x_ref[0], g_ref[...])


def final_rmsnorm(x, g, nb0, nbn):
    _, s, d = x.shape
    tm = _tile(s, 512, 8)
    return pl.pallas_call(
        _final_norm_kernel,
        out_shape=jax.ShapeDtypeStruct((nbn, s, d), F32),
        grid=(nbn, s // tm),
        in_specs=[pl.BlockSpec((1, tm, d), lambda b, i: (b + nb0, i, 0)),
                  pl.BlockSpec((1, d), lambda b, i: (0, 0))],
        out_specs=pl.BlockSpec((1, tm, d), lambda b, i: (b, i, 0)),
        compiler_params=_cp("parallel", "parallel"),
        name="final_norm",
    )(x, g.reshape(1, d))


def _mm_kernel(x_ref, w_ref, o_ref):
    o_ref[...] = _dot(x_ref[...], w_ref[...]).astype(o_ref.dtype)


def matmul(x, w, out_dtype, tm=1024, tn=1024):
    m, k = x.shape
    n = w.shape[1]
    tm = _tile(m, tm, 8)
    tn = _tile(n, tn, LANES)
    return pl.pallas_call(
        _mm_kernel,
        out_shape=jax.ShapeDtypeStruct((m, n), out_dtype),
        grid=(m // tm, n // tn),
        in_specs=[pl.BlockSpec((tm, k), lambda i, j: (i, 0)),
                  pl.BlockSpec((k, tn), lambda i, j: (0, j))],
        out_specs=pl.BlockSpec((tm, tn), lambda i, j: (i, j)),
        compiler_params=_cp("parallel", "parallel"),
        name="matmul",
    )(x, w)


def _mm_resid_kernel(h_ref, w_ref, x_ref, m_ref, o_ref, *, gate_row):
    gate = m_ref[0, gate_row:gate_row + 1, :]
    o_ref[...] = x_ref[...] + gate * _dot(h_ref[...], w_ref[...])


def matmul_resid(h, w, x, modt, gate_row, s, tm=1024, tn=1024):
    m, k = h.shape
    n = w.shape[1]
    tm = _tile(s, tm, 8)
    tn = _tile(n, tn, LANES)
    per = s // tm
    return pl.pallas_call(
        functools.partial(_mm_resid_kernel, gate_row=gate_row),
        out_shape=jax.ShapeDtypeStruct((m, n), F32),
        grid=(m // tm, n // tn),
        in_specs=[pl.BlockSpec((tm, k), lambda i, j: (i, 0)),
                  pl.BlockSpec((k, tn), lambda i, j: (0, j)),
                  pl.BlockSpec((tm, tn), lambda i, j: (i, j)),
                  pl.BlockSpec((1, 6, tn), lambda i, j: (i // per, 0, j))],
        out_specs=pl.BlockSpec((tm, tn), lambda i, j: (i, j)),
        input_output_aliases={2: 0},
        compiler_params=_cp("parallel", "parallel"),
        name="matmul_resid",
    )(h, w, x, modt)


def _mm_swiglu_kernel(h_ref, wg_ref, wu_ref, o_ref):
    h = h_ref[...]
    a = _dot(h, wg_ref[...])
    b = _dot(h, wu_ref[...])
    o_ref[...] = (_silu(a) * b).astype(o_ref.dtype)


def _mm_swiglu_scaled_kernel(h_ref, wg_ref, wu_ref, c_ref, o_ref, *, cols_per_expert):
    h = h_ref[...]
    a = _dot(h, wg_ref[...])
    b = _dot(h, wu_ref[...])
    tn = o_ref.shape[1]
    e = (pl.program_id(1) * tn) // cols_per_expert
    comb = c_ref[...]
    lane = lax.broadcasted_iota(jnp.int32, comb.shape, 1)
    sel = jnp.sum(jnp.where(lane == e, comb, 0.0), axis=1, keepdims=True)
    o_ref[...] = (_silu(a) * b * sel).astype(o_ref.dtype)


def matmul_swiglu(h, wg, wu, comb=None, cols_per_expert=None, tm=1024, tn=512):
    m, k = h.shape
    n = wg.shape[1]
    tm = _tile(m, tm, 8)
    tn = _tile(n if comb is None else cols_per_expert, tn, LANES)
    in_specs = [pl.BlockSpec((tm, k), lambda i, j: (i, 0)),
                pl.BlockSpec((k, tn), lambda i, j: (0, j)),
                pl.BlockSpec((k, tn), lambda i, j: (0, j))]
    args = [h, wg, wu]
    if comb is None:
        body = _mm_swiglu_kernel
    else:
        body = functools.partial(_mm_swiglu_scaled_kernel, cols_per_expert=cols_per_expert)
        in_specs.append(pl.BlockSpec((tm, comb.shape[1]), lambda i, j: (i, 0)))
        args.append(comb)
    return pl.pallas_call(
        body,
        out_shape=jax.ShapeDtypeStruct((m, n), BF16),
        grid=(m // tm, n // tn),
        in_specs=in_specs,
        out_specs=pl.BlockSpec((tm, tn), lambda i, j: (i, j)),
        compiler_params=_cp("parallel", "parallel"),
        name="matmul_swiglu",
    )(*args)


def _swap16(x):
    lane = lax.broadcasted_iota(jnp.int32, x.shape, 1)
    lo = (lane % 32) < 16
    return jnp.where(lo, pltpu.roll(x, LANES - 16, 1), pltpu.roll(x, 16, 1))


def _mm_rope_kernel(h_ref, w_ref, c_ref, s_ref, o_ref):
    acc = _dot(h_ref[...], w_ref[...]) * (MLA_SCALE * LOG2E)
    cs = c_ref[0]
    sn = s_ref[0]
    for blk in range(o_ref.shape[1] // LANES):
        x = acc[:, blk * LANES:(blk + 1) * LANES]
        if blk % 2 == 1:
            x = x * cs + _swap16(x) * sn
        o_ref[:, blk * LANES:(blk + 1) * LANES] = x.astype(o_ref.dtype)


def matmul_rope(h, w, cos_t, sin_t, s, nbp, tm=1024, tn=1024):
    m, k = h.shape
    n = w.shape[1]
    tm = _tile(s, tm, 8)
    tn = _tile(n, tn, 2 * LANES)
    per = s // tm
    t_spec = pl.BlockSpec((1, tm, LANES), lambda i, j: (jnp.where(i // per >= nbp, 1, 0), i % per, 0))
    return pl.pallas_call(
        _mm_rope_kernel,
        out_shape=jax.ShapeDtypeStruct((m, n), BF16),
        grid=(m // tm, n // tn),
        in_specs=[pl.BlockSpec((tm, k), lambda i, j: (i, 0)),
                  pl.BlockSpec((k, tn), lambda i, j: (0, j)),
                  t_spec, t_spec],
        out_specs=pl.BlockSpec((tm, tn), lambda i, j: (i, j)),
        compiler_params=_cp("parallel", "parallel"),
        name="matmul_rope",
    )(h, w, cos_t, sin_t)


def _gdn_prep_kernel(cur_ref, prev_ref, next_ref, w_ref, o_ref, pad_ref, *, seq_p, seq_s, nbp, d_model):
    b = pl.program_id(0)
    i = pl.program_id(1)
    j = pl.program_id(2)
    tm, tc = cur_ref.shape[1], cur_ref.shape[2]
    k_taps = w_ref.shape[0]
    half = k_taps // 2
    seqlen = jnp.where(b < nbp, seq_p, seq_s)
    row0 = i * tm
    at_start = (row0 % seqlen) == 0
    at_end = ((row0 + tm) % seqlen) == 0
    pad_ref[8:8 + tm, :] = cur_ref[0].astype(F32)
    pad_ref[0:8, :] = jnp.where(at_start, 0.0, prev_ref[0].astype(F32))
    pad_ref[8 + tm:16 + tm, :] = jnp.where(at_end, 0.0, next_ref[0].astype(F32))
    w = w_ref[...]
    acc = jnp.zeros((tm, tc), F32)
    for t in range(k_taps):
        acc = acc + pad_ref[8 - half + t:8 - half + t + tm, :] * w[t:t + 1, :]
    y = _silu(acc)
    col0 = j * tc
    is_qk = col0 < 2 * d_model
    qscale = jnp.where(col0 < d_model, HEAD_DIM ** -0.5, 1.0)
    for hb in range(tc // HEAD_DIM):
        yh = y[:, hb * HEAD_DIM:(hb + 1) * HEAD_DIM]
        ssq = jnp.sum(yh * yh, axis=1, keepdims=True)
        fac = jnp.where(is_qk, lax.rsqrt(ssq + EPS) * qscale, 1.0)
        o_ref[0, :, hb * HEAD_DIM:(hb + 1) * HEAD_DIM] = (yh * fac).astype(o_ref.dtype)


def gdn_prep(qkvz, conv_w, nb, s, seq_p, nbp):
    t, d4 = qkvz.shape
    d = d4 // 4
    x3 = qkvz.reshape(nb, s, d4)
    tm = _tile(min(s, seq_p), 256, 8)
    tc = _tile(d, 512, HEAD_DIM)
    r8 = tm // 8
    last8 = s // 8 - 1
    return pl.pallas_call(
        functools.partial(_gdn_prep_kernel, seq_p=seq_p, seq_s=s, nbp=nbp, d_model=d),
        out_shape=jax.ShapeDtypeStruct((nb, s, 3 * d), GDN_QKV_DTYPE),
        grid=(nb, s // tm, 3 * d // tc),
        in_specs=[
            pl.BlockSpec((1, tm, tc), lambda b, i, j: (b, i, j)),
            pl.BlockSpec((1, 8, tc), lambda b, i, j: (b, jnp.maximum(i * r8 - 1, 0), j)),
            pl.BlockSpec((1, 8, tc), lambda b, i, j: (b, jnp.minimum((i + 1) * r8, last8), j)),
            pl.BlockSpec((conv_w.shape[0], tc), lambda b, i, j: (0, j)),
        ],
        out_specs=pl.BlockSpec((1, tm, tc), lambda b, i, j: (b, i, j)),
        scratch_shapes=[pltpu.VMEM((tm + 16, tc), F32)],
        compiler_params=_cp("parallel", "parallel", "parallel"),
        name="gdn_prep",
    )(x3, x3, x3, conv_w)


def _split3(x):
    x1 = x.astype(BF16)
    r = x - x1.astype(F32)
    x2 = r.astype(BF16)
    x3 = (r - x2.astype(F32)).astype(BF16)
    return x1, x2, x3


def _gdn_gates_kernel(ab_ref, alog_ref, dt_ref, o_ref, oc_ref, *, n_heads):
    x = ab_ref[0]
    tm = x.shape[0]
    lane = lax.broadcasted_iota(jnp.int32, x.shape, 1)
    xa = x + dt_ref[...]
    sp = jnp.maximum(xa, 0.0) + jnp.log1p(jnp.exp(-jnp.abs(xa)))
    g = -jnp.exp(alog_ref[...]) * sp
    beta = _sigmoid(x)
    ri = lax.broadcasted_iota(jnp.int32, (tm, tm), 0)
    ci = lax.broadcasted_iota(jnp.int32, (tm, tm), 1)
    same = (ri // CHUNK) == (ci // CHUNK)
    m_pre = jnp.where(same & (ci <= ri), 1.0, 0.0).astype(BF16)
    m_suf = jnp.where(same & (ci >= ri), 1.0, 0.0).astype(BF16)
    pre = jnp.zeros_like(x)
    suf = jnp.zeros_like(x)
    for part in _split3(g):
        pre = pre + _dot(m_pre, part)
        suf = suf + _dot(m_suf, part)
    res = jnp.where(lane < n_heads, pre, jnp.where(lane < 2 * n_heads, suf, beta))
    o_ref[0] = res.T
    oc_ref[0] = res


def gdn_gates(ab, a_log, dt_bias, nb, s):
    t, w = ab.shape
    n_heads = w // 4
    tm = _tile(s, 256, CHUNK)
    pad = jnp.zeros((1, w - 2 * n_heads), F32)
    alog = jnp.concatenate([a_log.reshape(1, 2 * n_heads).astype(F32), pad], axis=1)
    dtb = jnp.concatenate([dt_bias.reshape(1, 2 * n_heads).astype(F32), pad], axis=1)
    return pl.pallas_call(
        functools.partial(_gdn_gates_kernel, n_heads=n_heads),
        out_shape=(jax.ShapeDtypeStruct((nb, w, s), F32),
                   jax.ShapeDtypeStruct((nb, s, w), F32)),
        grid=(nb, s // tm),
        in_specs=[pl.BlockSpec((1, tm, w), lambda b, i: (b, i, 0)),
                  pl.BlockSpec((1, w), lambda b, i: (0, 0)),
                  pl.BlockSpec((1, w), lambda b, i: (0, 0))],
        out_specs=(pl.BlockSpec((1, w, tm), lambda b, i: (b, 0, i)),
                   pl.BlockSpec((1, tm, w), lambda b, i: (b, i, 0))),
        compiler_params=_cp("parallel", "parallel"),
        name="gdn_gates",
    )(ab.reshape(nb, s, w), alog, dtb)


def _gdn_core_kernel(q_ref, k_ref, v_ref, g_ref, gc_ref, s0_ref, o_ref, sfin_ref,
                     state_sc, wq_sc, u_sc, kd_sc, intra_sc, gl_sc, *, hg, nc, n_heads):
    dirn = pl.program_id(0)
    blk = pl.program_id(3)
    nblk = pl.num_programs(3)
    rev = dirn == 1
    c = CHUNK

    @pl.when(blk == 0)
    def _():
        state_sc[...] = s0_ref[0, 0]

    ii = lax.broadcasted_iota(jnp.int32, (c, c), 0)
    jj = lax.broadcasted_iota(jnp.int32, (c, c), 1)
    dmat = jnp.where(rev, jj - ii, ii - jj)
    incl = dmat >= 0
    strict = dmat > 0

    rows = g_ref[0, 0, 0]
    w4 = 4 * n_heads
    lane0 = dirn * n_heads + pl.program_id(2) * hg

    qs, ks, vs, grows = [], [], [], []
    c_g, c_b, c_eg, c_beg, c_kdm, c_gl = [], [], [], [], [], []
    for cc in range(nc):
        r0 = cc * c
        gt = gc_ref[0, r0:r0 + c, :]
        gg = pltpu.roll(gt, (w4 - lane0) % w4, 1)
        bb = pltpu.roll(gt, 2 * n_heads - lane0, 1)
        eg = jnp.exp(gg)
        beg = bb * eg
        glast = jnp.where(rev, gg[0:1], gg[c - 1:c])
        kdm = jnp.exp(glast - gg)
        glt = jnp.exp(glast)
        for hh in range(hg):
            cs = slice(hh * HEAD_DIM, (hh + 1) * HEAD_DIM)
            sl = slice(hh, hh + 1)
            qs.append(q_ref[r0:r0 + c, cs])
            ks.append(k_ref[r0:r0 + c, cs])
            vs.append(v_ref[r0:r0 + c, cs])
            grows.append(rows[hh:hh + 1, r0:r0 + c])
            c_g.append(gg[:, sl])
            c_b.append(bb[:, sl])
            c_eg.append(eg[:, sl])
            c_beg.append(beg[:, sl])
            c_kdm.append(kdm[:, sl])
            c_gl.append(glt[:, sl])
    q = jnp.stack(qs)
    k = jnp.stack(ks)
    qf = q.astype(F32)
    kf = k.astype(F32)
    vf = jnp.stack(vs).astype(F32)
    gcol = jnp.stack(c_g)
    bcol = jnp.stack(c_b)
    grow = jnp.stack(grows)
    decay = jnp.where(incl, jnp.exp(jnp.where(incl, gcol - grow, 0.0)), 0.0)
    kb = k.astype(BF16)
    qkk = _bdot_nt(jnp.concatenate([q.astype(BF16), kb], axis=1), kb)
    intra = qkk[:, :c] * decay
    nmat = jnp.where(strict, qkk[:, c:] * bcol * decay, 0.0)
    def bb(x):
        return x.astype(BF16)

    blk_i = ii // SOLVE_BASE
    blk_j = jj // SOLVE_BASE
    m0 = jnp.where(blk_i == blk_j, -nmat, 0.0)
    q1 = _bdot(bb(m0), bb(m0))
    q1b = bb(q1)
    a1 = m0 + q1 + _bdot(bb(m0), q1b)
    q2 = _bdot(q1b, q1b)
    r = a1 + q2 + _bdot(bb(a1), bb(q2))
    width = SOLVE_BASE
    while width < c:
        pair = (ii // (2 * width)) == (jj // (2 * width))
        noff = jnp.where(pair & ((ii // width) != (jj // width)), nmat, 0.0)
        y = noff + _bdot(bb(r), bb(noff))
        r = r - (y + _bdot(bb(y), bb(r)))
        width *= 2
    rhs = jnp.concatenate([vf * bcol, kf * jnp.stack(c_beg)], axis=2)
    sol = rhs + _bdot(r.astype(BF16), rhs.astype(BF16))
    u_sc[...] = sol[:, :, :HEAD_DIM]
    wq_sc[:, 0:c, :] = sol[:, :, HEAD_DIM:].astype(BF16)
    wq_sc[:, c:2 * c, :] = (qf * jnp.stack(c_eg)).astype(BF16)
    kd_sc[...] = (kf * jnp.stack(c_kdm)).astype(BF16)
    intra_sc[...] = intra.astype(BF16)
    gl_sc[...] = jnp.broadcast_to(jnp.stack(c_gl), gl_sc.shape)

    for step in range(nc):
        cc = jnp.where(rev, nc - 1 - step, step)
        r0 = pl.multiple_of(cc * c, c)
        ps = pl.ds(cc * hg, hg)
        st = state_sc[...]
        ws = _bdot(wq_sc[ps], st.astype(BF16))
        v_new = u_sc[ps] - ws[:, :c]
        vb = v_new.astype(BF16)
        o = ws[:, c:] + _bdot(intra_sc[ps], vb)
        state_sc[...] = st * gl_sc[ps] + _bdot_tn(kd_sc[ps], vb)
        for hh in range(hg):
            o_ref[0, pl.ds(r0, c), hh * HEAD_DIM:(hh + 1) * HEAD_DIM] = o[hh].astype(o_ref.dtype)

    @pl.when(blk == nblk - 1)
    def _():
        sfin_ref[0, 0] = state_sc[...]


def gdn_core(qkv, gates, gates_c, s0, row0, n_seq, seq_len, s, o_prev=None):
    t, d3 = qkv.shape
    d = d3 // 3
    n_heads = d // HEAD_DIM
    hg = GDN_HEADS_PER_STEP if n_heads % GDN_HEADS_PER_STEP == 0 else n_heads
    ngrp = n_heads // hg
    lblk = _tile(seq_len, GDN_BLOCK_ROWS, 2 * CHUNK)
    nc = lblk // CHUNK
    nblk = seq_len // lblk
    wcols = hg * HEAD_DIM
    kofs = d // wcols
    rb0 = row0 // lblk
    per_s = s // lblk
    g5 = gates.reshape(gates.shape[0], 4, ngrp, hg, s)

    def eff(dirn, bk):
        return jnp.where(dirn == 1, nblk - 1 - bk, bk)

    def row_blk(dirn, q, bk):
        return rb0 + q * nblk + eff(dirn, bk)

    def qkv_spec(sec):
        return pl.BlockSpec((lblk, wcols), lambda dr, q, g, bk: (row_blk(dr, q, bk), sec * kofs + g))

    def grow_map(dr, q, g, bk):
        rb = row_blk(dr, q, bk)
        return (rb // per_s, dr, g, 0, rb % per_s)

    def gcol_map(dr, q, g, bk):
        rb = row_blk(dr, q, bk)
        return (rb // per_s, rb % per_s, 0)

    grow_spec = pl.BlockSpec((1, 1, 1, hg, lblk), grow_map)
    gcol_spec = pl.BlockSpec((1, lblk, 4 * n_heads), gcol_map)
    st_spec = pl.BlockSpec((1, 1, hg, HEAD_DIM, HEAD_DIM), lambda dr, q, g, bk: (q, dr, g, 0, 0))
    o_spec = pl.BlockSpec((1, lblk, wcols), lambda dr, q, g, bk: (dr, row_blk(dr, q, bk), g))
    in_specs = [qkv_spec(0), qkv_spec(1), qkv_spec(2), grow_spec, gcol_spec, st_spec]
    args = [qkv, qkv, qkv, g5, gates_c, s0]
    aliases = {}
    body = functools.partial(_gdn_core_kernel, hg=hg, nc=nc, n_heads=n_heads)
    if o_prev is not None:
        in_specs.append(pl.BlockSpec(memory_space=pl.ANY))
        args.append(o_prev)
        aliases = {6: 0}
        inner = body

        def body(q_ref, k_ref, v_ref, g_ref, b_ref, s0_ref, _prev, *rest):
            inner(q_ref, k_ref, v_ref, g_ref, b_ref, s0_ref, *rest)

    return pl.pallas_call(
        body,
        out_shape=(jax.ShapeDtypeStruct((2, t, d), GDN_O_DTYPE),
                   jax.ShapeDtypeStruct((n_seq, 2, n_heads, HEAD_DIM, HEAD_DIM), F32)),
        grid=(2, n_seq, ngrp, nblk),
        in_specs=in_specs,
        out_specs=(o_spec, st_spec),
        scratch_shapes=[
            pltpu.VMEM((hg, HEAD_DIM, HEAD_DIM), F32),
            pltpu.VMEM((nc * hg, 2 * CHUNK, HEAD_DIM), BF16),
            pltpu.VMEM((nc * hg, CHUNK, HEAD_DIM), F32),
            pltpu.VMEM((nc * hg, CHUNK, HEAD_DIM), BF16),
            pltpu.VMEM((nc * hg, CHUNK, CHUNK), BF16),
            pltpu.VMEM((nc * hg, 1, HEAD_DIM), F32),
        ],
        input_output_aliases=aliases,
        compiler_params=_cp("parallel", "parallel", "parallel", "arbitrary"),
        name="gdn_core",
    )(*args)


def _gdn_post_kernel(of_ref, ob_ref, z_ref, nw_ref, o_ref):
    o = of_ref[0].astype(F32) + ob_ref[0].astype(F32)
    z = z_ref[...].astype(F32)
    nw = nw_ref[...]
    for hb in range(o.shape[1] // HEAD_DIM):
        cs = slice(hb * HEAD_DIM, (hb + 1) * HEAD_DIM)
        oh = o[:, cs]
        var = jnp.mean(oh * oh, axis=1, keepdims=True)
        y = oh * lax.rsqrt(var + EPS) * nw
        o_ref[:, cs] = (y * _silu(z[:, cs])).astype(o_ref.dtype)


def gdn_post(o2, qkvz, norm_w):
    _, t, d = o2.shape
    tm = _tile(t, 512, 8)
    tc = _tile(d, 1024, HEAD_DIM)
    zofs = 3 * d // tc
    return pl.pallas_call(
        _gdn_post_kernel,
        out_shape=jax.ShapeDtypeStruct((t, d), BF16),
        grid=(t // tm, d // tc),
        in_specs=[pl.BlockSpec((1, tm, tc), lambda i, j: (0, i, j)),
                  pl.BlockSpec((1, tm, tc), lambda i, j: (1, i, j)),
                  pl.BlockSpec((tm, tc), lambda i, j: (i, zofs + j)),
                  pl.BlockSpec((1, HEAD_DIM), lambda i, j: (0, 0))],
        out_specs=pl.BlockSpec((tm, tc), lambda i, j: (i, j)),
        compiler_params=_cp("parallel", "parallel"),
        name="gdn_post",
    )(o2, o2, qkvz, norm_w.reshape(1, HEAD_DIM).astype(F32))


def _mla_prep_kernel(p_ref, qn_ref, kvn_ref, c_ref, s_ref, cq_ref, kv_ref, *, q_lora, kv_lora):
    p = p_ref[...]
    cq_ref[...] = _rms(p[:, :q_lora], qn_ref[...]).astype(cq_ref.dtype)
    kv_ref[:, :kv_lora] = _rms(p[:, q_lora:q_lora + kv_lora], kvn_ref[...])
    kr = p[:, q_lora + kv_lora:]
    kv_ref[:, kv_lora:] = kr * c_ref[0] + _swap16(kr) * s_ref[0]


def mla_prep(proj, q_norm, kv_norm, cos_t, sin_t, s, nbp, q_lora, kv_lora):
    t, w = proj.shape
    tm = _tile(s, 512, 8)
    per = s // tm
    t_spec = pl.BlockSpec((1, tm, LANES), lambda i: (jnp.where(i // per >= nbp, 1, 0), i % per, 0))
    return pl.pallas_call(
        functools.partial(_mla_prep_kernel, q_lora=q_lora, kv_lora=kv_lora),
        out_shape=(jax.ShapeDtypeStruct((t, q_lora), BF16),
                   jax.ShapeDtypeStruct((t, kv_lora + LANES), F32)),
        grid=(t // tm,),
        in_specs=[pl.BlockSpec((tm, w), lambda i: (i, 0)),
                  pl.BlockSpec((1, q_lora), lambda i: (0, 0)),
                  pl.BlockSpec((1, kv_lora), lambda i: (0, 0)),
                  t_spec, t_spec],
        out_specs=(pl.BlockSpec((tm, q_lora), lambda i: (i, 0)),
                   pl.BlockSpec((tm, kv_lora + LANES), lambda i: (i, 0))),
        compiler_params=_cp("parallel"),
        name="mla_prep",
    )(proj, q_norm.reshape(1, q_lora).astype(F32), kv_norm.reshape(1, kv_lora).astype(F32), cos_t, sin_t)


def _mla_attn_kernel(q_ref, kv_ref, kr_ref, o_ref, kf_sc, vf_sc):
    hp = kf_sc.shape[0]
    w2 = 2 * HEAD_DIM

    @pl.when(pl.program_id(2) == 0)
    def _():
        for j in range(hp):
            kf_sc[j, :, :HEAD_DIM] = kv_ref[:, j * w2:j * w2 + HEAD_DIM]
            kf_sc[j, :, HEAD_DIM:] = kr_ref[...]
            vf_sc[j, :, :HEAD_DIM] = kv_ref[:, j * w2 + HEAD_DIM:(j + 1) * w2]
            vf_sc[j, :, HEAD_DIM:] = jnp.ones((vf_sc.shape[1], HEAD_DIM), BF16)

    tq = q_ref.shape[0]
    sub = tq // MLA_Q_SUBBLOCKS
    for j in range(hp):
        ss = [_dot_nt(q_ref[i * sub:(i + 1) * sub, j * w2:(j + 1) * w2], kf_sc[j]) for i in range(MLA_Q_SUBBLOCKS)]
        ps = [jnp.exp2(s - jnp.max(s, axis=1, keepdims=True)).astype(BF16) for s in ss]
        for i, p in enumerate(ps):
            ol = _dot(p, vf_sc[j])
            o_ref[i * sub:(i + 1) * sub, j * HEAD_DIM:(j + 1) * HEAD_DIM] = (
                ol[:, :HEAD_DIM] / ol[:, HEAD_DIM:HEAD_DIM + 1]).astype(o_ref.dtype)


def mla_attention(q, kvh, kr, qrow0, n_seq, lq, kvrow0, lk, o_prev=None):
    t = q.shape[0]
    n_heads = q.shape[1] // (2 * HEAD_DIM)
    tq = _tile(lq, MLA_Q_ROWS, 8)
    nq = lq // tq
    qb0 = qrow0 // tq
    kb0 = kvrow0 // lk
    hp = max(1, min(n_heads, MLA_HEAD_KEYS // lk))
    while n_heads % hp:
        hp -= 1
    in_specs = [
        pl.BlockSpec((tq, hp * 2 * HEAD_DIM), lambda b, h, i: (qb0 + b * nq + i, h)),
        pl.BlockSpec((lk, hp * 2 * HEAD_DIM), lambda b, h, i: (kb0 + b, h)),
        pl.BlockSpec((lk, LANES), lambda b, h, i: (kb0 + b, 0)),
    ]
    args = [q, kvh, kr]
    body = _mla_attn_kernel
    aliases = {}
    if o_prev is not None:
        in_specs.append(pl.BlockSpec(memory_space=pl.ANY))
        args.append(o_prev)
        aliases = {3: 0}

        def body(q_ref, kv_ref, kr_ref, _prev, o_ref, kf_sc, vf_sc):
            _mla_attn_kernel(q_ref, kv_ref, kr_ref, o_ref, kf_sc, vf_sc)

    return pl.pallas_call(
        body,
        out_shape=jax.ShapeDtypeStruct((t, n_heads * HEAD_DIM), BF16),
        grid=(n_seq, n_heads // hp, nq),
        in_specs=in_specs,
        out_specs=pl.BlockSpec((tq, hp * HEAD_DIM), lambda b, h, i: (qb0 + b * nq + i, h)),
        scratch_shapes=[pltpu.VMEM((hp, lk, 2 * HEAD_DIM), BF16), pltpu.VMEM((hp, lk, 2 * HEAD_DIM), BF16)],
        input_output_aliases=aliases,
        compiler_params=_cp("parallel", "parallel", "arbitrary"),
        name="mla_attention",
    )(*args)


def _rope_tables(s):
    rows = s // GRID_W
    r = jnp.repeat(jnp.arange(rows), GRID_W).astype(F32)
    col = jnp.tile(jnp.arange(GRID_W), rows).astype(F32)
    inv = ROPE_THETA ** (-jnp.arange(0, AXIS_ROT, 2, dtype=F32) / AXIS_ROT)
    ar, ac = r[:, None] * inv, col[:, None] * inv
    zeros = jnp.zeros((s, LANES - ROPE_DIM), F32)
    cos = jnp.concatenate([jnp.cos(ar), jnp.cos(ar), jnp.cos(ac), jnp.cos(ac), zeros], axis=1)
    sin = jnp.concatenate([-jnp.sin(ar), jnp.sin(ar), -jnp.sin(ac), jnp.sin(ac), zeros], axis=1)
    ident_c = jnp.concatenate([jnp.ones((s, ROPE_DIM), F32), zeros], axis=1)
    return jnp.stack([ident_c, cos]), jnp.stack([jnp.zeros((s, LANES), F32), sin])


def kernel(x_prompt, x_sample, state_gdn, cache_mla, c, c_ctx, ada_w, ada_b, norm1, norm2, gdn_w_in, gdn_conv, gdn_a_log, gdn_dt_bias, gdn_norm, gdn_w_out, mla_w_in, mla_q_norm, mla_w_qb, mla_kv_norm, mla_w_kvb, mla_w_out, ffn_w_gate, ffn_w_up, ffn_w_down, moe_router, moe_w_gate, moe_w_up, moe_w_down, final_norm):
    batch, seq, d = x_prompt.shape
    dec_batch, s, _ = x_sample.shape
    depth = ada_w.shape[0]
    assert (batch * seq) % s == 0 and seq % (2 * CHUNK) == 0 and s % GRID_W == 0
    nbp = batch * seq // s
    nb = nbp + dec_batch
    t = nb * s
    tp = nbp * s
    n_heads = d // HEAD_DIM
    past = cache_mla.shape[2]
    q_lora = mla_q_norm.shape[1]
    kv_lora = mla_kv_norm.shape[1]
    lk_s = past + s

    x = jnp.concatenate([x_prompt.reshape(nbp, s, d), x_sample], axis=0).reshape(t, d)

    cvec = jnp.concatenate([jnp.broadcast_to(c_ctx[None], (nbp, d)), c], axis=0)
    rpad = -nb % 8
    cvec = jnp.pad(cvec, ((0, rpad), (0, 0)))
    mod_all = adaln_all(cvec, ada_w, ada_b)[:, :nb].reshape(depth, nb, 6, d)

    cos_t, sin_t = _rope_tables(s)
    zero_state = jnp.zeros((batch, 2, n_heads, HEAD_DIM, HEAD_DIM), F32)
    new_gdn, new_mla = [], []

    for l in range(depth):
        modt = mod_all[l]
        j = l // 2
        h = norm_mod(x.reshape(nb, s, d), norm1[l], modt, 0).reshape(t, d)
        if l % 2 == 0:
            w_in = gdn_w_in[j].astype(BF16)
            qkvz = matmul(h, w_in[:, :4 * d], GDN_PROJ_DTYPE)
            ab = matmul(h, w_in[:, 4 * d:], F32)
            qkv = gdn_prep(qkvz, gdn_conv[j], nb, s, seq, nbp).reshape(t, 3 * d)
            gates, gates_c = gdn_gates(ab, gdn_a_log[j], gdn_dt_bias[j], nb, s)
            o2, st = gdn_core(qkv, gates, gates_c, zero_state, 0, batch, seq, s)
            o2, _ = gdn_core(qkv, gates, gates_c, state_gdn[:, j], tp, dec_batch, s, s, o_prev=o2)
            new_gdn.append(st)
            mix = gdn_post(o2, qkvz, gdn_norm[j])
            w_out = gdn_w_out[j].astype(BF16)
        else:
            w_in = jnp.pad(mla_w_in[j], ((0, 0), (0, LANES - ROPE_DIM))).astype(BF16)
            proj = matmul(h, w_in, F32, tm=512, tn=w_in.shape[1])
            cq, kv_ent = mla_prep(proj, mla_q_norm[j], mla_kv_norm[j], cos_t, sin_t, s, nbp, q_lora, kv_lora)
            new_mla.append(kv_ent[:tp, :kv_lora + ROPE_DIM].reshape(batch, seq, kv_lora + ROPE_DIM))
            wq = mla_w_qb[j].reshape(q_lora, n_heads, HEAD_DIM + ROPE_DIM)
            wq = jnp.pad(wq, ((0, 0), (0, 0), (0, HEAD_DIM - ROPE_DIM))).reshape(q_lora, n_heads * 2 * HEAD_DIM)
            qf = matmul_rope(cq, wq.astype(BF16), cos_t, sin_t, s, nbp)
            cache = jnp.pad(cache_mla[:, j], ((0, 0), (0, 0), (0, LANES - ROPE_DIM)))
            kv_s = jnp.concatenate([cache, kv_ent[tp:].reshape(dec_batch, s, kv_lora + LANES)], axis=1)
            kv_all = jnp.concatenate([kv_s.reshape(dec_batch * lk_s, kv_lora + LANES), kv_ent[:tp]], axis=0)
            kv_all = kv_all.astype(BF16)
            kvh = matmul(kv_all[:, :kv_lora], mla_w_kvb[j].astype(BF16), BF16)
            kr = kv_all[:, kv_lora:]
            att = mla_attention(qf, kvh, kr, 0, batch, seq, dec_batch * lk_s, seq)
            mix = mla_attention(qf, kvh, kr, tp, dec_batch, s, 0, lk_s, o_prev=att)
            w_out = mla_w_out[j].astype(BF16)
        x = matmul_resid(mix, w_out, x, modt, 2, s)

        if l % 2 == 0:
            h = norm_mod(x.reshape(nb, s, d), norm2[l], modt, 3).reshape(t, d)
            mid = matmul_swiglu(h, ffn_w_gate[j].astype(BF16), ffn_w_up[j].astype(BF16))
            w_down = ffn_w_down[j].astype(BF16)
        else:
            h, comb = norm_mod(x.reshape(nb, s, d), norm2[l], modt, 3, router=moe_router[j])
            h = h.reshape(t, d)
            n_exp, _, fe = moe_w_gate[j].shape
            wg = moe_w_gate[j].transpose(1, 0, 2).reshape(d, n_exp * fe).astype(BF16)
            wu = moe_w_up[j].transpose(1, 0, 2).reshape(d, n_exp * fe).astype(BF16)
            mid = matmul_swiglu(h, wg, wu, comb=comb.reshape(t, n_exp), cols_per_expert=fe)
            w_down = moe_w_down[j].reshape(n_exp * fe, d).astype(BF16)
        x = matmul_resid(mid, w_down, x, modt, 5, s, tm=1024 if mid.shape[1] <= 6144 else 512, tn=512)

    x3 = x.reshape(nb, s, d)
    y_prompt = final_rmsnorm(x3, final_norm, 0, nbp).reshape(batch, seq, d)
    y_sample = final_rmsnorm(x3, final_norm, nbp, dec_batch)
    return (y_prompt, y_sample, jnp.stack(new_gdn, axis=1), jnp.stack(new_mla, axis=1))
```
